```python
import math
import jax, jax.numpy as jnp
from jax import lax
import numpy as np

D_MODEL = 1024
BATCH = 4
SEQ = 8192
DEPTH = 2
DEC_BATCH = 128
DEC_SEQ = 1
PAST_LEN = 16384
PAGE_SIZE = 128

A_HEADS = 8
A_KV_HEADS = 2
A_HEAD_DIM = 64
A_GROUP = A_HEADS // A_KV_HEADS
WINDOW = 128
N_BUCKETS = 32
MAX_DISTANCE = 128
A_Q_COLS = A_HEADS * A_HEAD_DIM
A_KV_COLS = A_KV_HEADS * A_HEAD_DIM
POOL_WINDOWS = (2, 4, 8, 16)
POOL_GROUPS = 4
POOL_GROUP_W = 128
POOL_W = POOL_GROUPS * POOL_GROUP_W
POOL_BUF = 15
C_HEADS = 4
C_HEAD_DIM = 128
C_W = C_HEADS * C_HEAD_DIM
C_CHUNK = 128
N_BRANCH = 3
IN_COLS = A_Q_COLS + 2 * A_KV_COLS + POOL_W + 4 * C_W + 2 * C_HEADS + N_BRANCH * D_MODEL
PEER_HEADS = 8
PEER_DK = 256
N_KEYS = 128
N_EXPERTS = N_KEYS * N_KEYS
PEER_TOPK = 16
PEER_BLOCK = 256
EPS = 1e-6

kernel_name = "hybrid_swa_pool_mlstm_peer_step"


def _split_points():
    sizes = (A_Q_COLS, A_KV_COLS, A_KV_COLS, POOL_W, C_W, C_W, C_W, C_HEADS, C_HEADS, C_W,
             D_MODEL, D_MODEL, D_MODEL)
    pts, acc = [], 0
    for s in sizes[:-1]:
        acc += s
        pts.append(acc)
    return pts


def _rmsnorm(x, g):
    xf = x.astype(jnp.float32)
    y = xf * lax.rsqrt(jnp.mean(xf * xf, axis=-1, keepdims=True) + EPS)
    return (y * g.astype(jnp.float32)).astype(x.dtype)


def _rel_bucket(dist):
    max_exact = N_BUCKETS // 2
    d = jnp.maximum(dist, 0)
    df = jnp.maximum(d, 1).astype(jnp.float32)
    large = max_exact + (jnp.log(df / max_exact) / math.log(MAX_DISTANCE / max_exact)
                         * (N_BUCKETS - max_exact)).astype(jnp.int32)
    large = jnp.minimum(large, N_BUCKETS - 1)
    return jnp.where(d < max_exact, d, large)


def _sink_attention(q, k, v, dist, valid, sink, rel_bias):
    Tq, Tk = dist.shape
    logits = jnp.einsum('...qkgd,...skd->...kgqs', q, k).astype(jnp.float32) * (A_HEAD_DIM ** -0.5)
    bias = jnp.moveaxis(rel_bias[_rel_bucket(dist)].astype(jnp.float32), -1, 0)
    bias = bias.reshape(A_KV_HEADS, A_GROUP, Tq, Tk)
    logits = jnp.where(valid, logits + bias, -jnp.inf)
    sink_l = sink.astype(jnp.float32).reshape(A_KV_HEADS, A_GROUP, 1, 1)
    mx = jnp.maximum(jnp.max(logits, axis=-1, keepdims=True), sink_l)
    p = jnp.exp(logits - mx)
    p = p / (jnp.sum(p, axis=-1, keepdims=True) + jnp.exp(sink_l - mx))
    return jnp.einsum('...kgqs,...skd->...qkgd', p.astype(v.dtype), v)


def _swa_prompt(q, k, v, sink, rel_bias):
    N, T = q.shape[:2]
    nb = T // WINDOW
    qb = q.reshape(N, nb, WINDOW, A_KV_HEADS, A_GROUP, A_HEAD_DIM)
    pad = ((0, 0), (WINDOW, 0), (0, 0), (0, 0))
    kp = jnp.pad(k, pad).reshape(N, nb + 1, WINDOW, A_KV_HEADS, A_HEAD_DIM)
    vp = jnp.pad(v, pad).reshape(N, nb + 1, WINDOW, A_KV_HEADS, A_HEAD_DIM)
    kb = jnp.concatenate([kp[:, :-1], kp[:, 1:]], axis=2)
    vb = jnp.concatenate([vp[:, :-1], vp[:, 1:]], axis=2)
    dist = jnp.arange(WINDOW)[:, None] + WINDOW - jnp.arange(2 * WINDOW)[None, :]
    kpos = jnp.arange(nb)[:, None] * WINDOW - WINDOW + jnp.arange(2 * WINDOW)[None, :]
    valid = (dist >= 0)[None] & (dist < WINDOW)[None] & (kpos >= 0)[:, None, :]
    out = _sink_attention(qb, kb, vb, dist, valid[:, None, None], sink, rel_bias)
    return out.reshape(N, T, A_Q_COLS), k[:, -WINDOW:], v[:, -WINDOW:]


def _swa_sample(q, k, v, k_buf, v_buf, sink, rel_bias):
    N, T = q.shape[:2]
    kk = jnp.concatenate([k_buf.astype(k.dtype), k], axis=1)
    vv = jnp.concatenate([v_buf.astype(v.dtype), v], axis=1)
    dist = (WINDOW + jnp.arange(T))[:, None] - jnp.arange(WINDOW + T)[None, :]
    valid = (dist >= 0) & (dist < WINDOW)
    out = _sink_attention(q, kk, vv, dist, valid, sink, rel_bias)
    return out.reshape(N, T, A_Q_COLS), kk[:, T:], vv[:, T:]


def _pool_mixer(xp, buf, pos0, w_pool, pool_scale):
    T = xp.shape[1]
    full = jnp.concatenate([buf.astype(xp.dtype), xp], axis=1)
    cs = jnp.cumsum(full.astype(jnp.float32), axis=1)
    cs = jnp.concatenate([jnp.zeros_like(cs[:, :1]), cs], axis=1)
    end = cs[:, POOL_BUF + 1:]
    pos = pos0 + jnp.arange(T)
    outs = []
    for g, w in enumerate(POOL_WINDOWS):
        sl = slice(g * POOL_GROUP_W, (g + 1) * POOL_GROUP_W)
        start = cs[:, POOL_BUF + 1 - w: POOL_BUF + 1 - w + T, sl]
        cnt = jnp.minimum(pos + 1, w).astype(jnp.float32)[None, :, None]
        outs.append((end[..., sl] - start) / cnt - full[:, POOL_BUF:, sl].astype(jnp.float32))
    pooled = jnp.stack(outs, axis=2)
    y = jnp.einsum('ntgc,gcd->ntgd', pooled.astype(xp.dtype), w_pool).reshape(xp.shape)
    return y * pool_scale, full[:, T:]


def _mlstm_chunk(carry, inp):
    C, n, m = carry
    q, k, v, ig, lf = inp
    L = q.shape[2]
    b = jnp.cumsum(lf, axis=-1)
    causal = jnp.tril(jnp.ones((L, L), dtype=bool))
    D = jnp.where(causal, b[..., :, None] - b[..., None, :] + ig[..., None, :], -jnp.inf)
    inter = b + m[..., None]
    m_t = jnp.maximum(inter, jnp.max(D, axis=-1))
    S = jnp.einsum('nhld,nhsd->nhls', q, k) * jnp.exp(D - m_t[..., None])
    w_inter = jnp.exp(inter - m_t)
    num = jnp.einsum('nhls,nhsd->nhld', S, v) + w_inter[..., None] * jnp.einsum('nhed,nhld->nhle', C, q)
    den = jnp.sum(S, axis=-1) + w_inter * jnp.einsum('nhd,nhld->nhl', n, q)
    h = num / jnp.maximum(jnp.abs(den), jnp.exp(-m_t))[..., None]
    m_new = m_t[..., -1]
    w_end = jnp.exp(b[..., -1:] - b + ig - m_new[..., None])
    decay = jnp.exp(b[..., -1] + m - m_new)
    C_new = decay[..., None, None] * C + jnp.einsum('nhl,nhle,nhld->nhed', w_end, v, k)
    n_new = decay[..., None] * n + jnp.einsum('nhl,nhld->nhd', w_end, k)
    return (C_new, n_new, m_new), h


def _mlstm(q, k, v, ig, lf, C0, n0, m0):
    N, T, H, d = q.shape
    L = C_CHUNK if T % C_CHUNK == 0 else T
    nc = T // L

    def chunks(a):
        a = a.astype(jnp.float32).reshape((N, nc, L, H) + a.shape[3:])
        return jnp.moveaxis(a, (1, 3), (0, 2))

    carry0 = (C0.astype(jnp.float32), n0.astype(jnp.float32), m0.astype(jnp.float32))
    (C, n, m), h = lax.scan(_mlstm_chunk, carry0, (chunks(q), chunks(k), chunks(v), chunks(ig), chunks(lf)))
    h = jnp.moveaxis(h, (0, 2), (1, 3)).reshape(N, T, H, d)
    return h, C, n, m


def _peer(x, wq, k1, k2, u, v):
    N, T, D = x.shape
    n_tok = N * T
    blk = min(PEER_BLOCK, n_tok)
    n_blk = -(-n_tok // blk)
    xt = jnp.pad(x.reshape(n_tok, D), ((0, n_blk * blk - n_tok), (0, 0)))

    def block(xb):
        q = (xb @ wq).reshape(blk, PEER_HEADS, 2, PEER_DK // 2)
        s1 = jnp.einsum('thd,kd->thk', q[:, :, 0], k1).astype(jnp.float32)
        s2 = jnp.einsum('thd,kd->thk', q[:, :, 1], k2).astype(jnp.float32)
        v1, i1 = lax.top_k(s1, PEER_TOPK)
        v2, i2 = lax.top_k(s2, PEER_TOPK)
        cand = (v1[..., :, None] + v2[..., None, :]).reshape(blk, PEER_HEADS, PEER_TOPK * PEER_TOPK)
        cidx = (i1[..., :, None] * N_KEYS + i2[..., None, :]).reshape(blk, PEER_HEADS, PEER_TOPK * PEER_TOPK)
        top_v, top_pos = lax.top_k(cand, PEER_TOPK)
        experts = jnp.take_along_axis(cidx, top_pos, axis=-1)
        gates = jax.nn.softmax(top_v, axis=-1)
        act = jax.nn.gelu(jnp.einsum('td,thkd->thk', xb, u[experts]).astype(jnp.float32), approximate=False)
        return jnp.einsum('thk,thkd->td', (gates * act).astype(xb.dtype), v[experts])

    y = lax.map(block, xt.reshape(n_blk, blk, D))
    return y.reshape(n_blk * blk, D)[:n_tok].reshape(N, T, D)


def _layer(x, is_prompt, st, p, rel_bias):
    N, T, _ = x.shape
    xn = _rmsnorm(x, p['norm1_g'])
    z = xn @ p['w_in']
    aq, ak, av, pin, cq, ck, cv, ci, cf, co, ga, gb, gc = jnp.split(z, _split_points(), axis=-1)
    q = _rmsnorm(aq.reshape(N, T, A_KV_HEADS, A_GROUP, A_HEAD_DIM), p['q_norm_g'])
    k = _rmsnorm(ak.reshape(N, T, A_KV_HEADS, A_HEAD_DIM), p['k_norm_g'])
    v = av.reshape(N, T, A_KV_HEADS, A_HEAD_DIM)
    if is_prompt:
        ya, k_buf, v_buf = _swa_prompt(q, k, v, p['sink'], rel_bias)
        pos0 = 0
    else:
        ya, k_buf, v_buf = _swa_sample(q, k, v, st['k'], st['v'], p['sink'], rel_bias)
        pos0 = PAST_LEN
    yb, pool_buf = _pool_mixer(pin, st['pool'], pos0, p['w_pool'], p['pool_scale'])
    qc = cq.reshape(N, T, C_HEADS, C_HEAD_DIM)
    kc = ck.reshape(N, T, C_HEADS, C_HEAD_DIM) * (C_HEAD_DIM ** -0.5)
    vc = cv.reshape(N, T, C_HEADS, C_HEAD_DIM)
    ig = ci.astype(jnp.float32) + p['b_igate'].astype(jnp.float32)
    lf = jax.nn.log_sigmoid(cf.astype(jnp.float32) + p['b_fgate'].astype(jnp.float32))
    h, C, n, m = _mlstm(qc, kc, vc, ig, lf, st['c'], st['n'], st['m'])
    h = _rmsnorm(h.astype(x.dtype), p['h_norm_g'])
    yc = (h * jax.nn.sigmoid(co).reshape(N, T, C_HEADS, C_HEAD_DIM)).reshape(N, T, C_W)
    merged = (jax.nn.sigmoid(ga) * (ya @ p['w_br_a']) + jax.nn.sigmoid(gb) * (yb @ p['w_br_b'])
              + jax.nn.sigmoid(gc) * (yc @ p['w_br_c']))
    x = x + merged @ p['w_out']
    x = x + _peer(_rmsnorm(x, p['norm2_g']), p['peer_wq'], p['peer_k1'], p['peer_k2'], p['peer_u'], p['peer_v'])
    return x, (k_buf, v_buf, pool_buf, C, n, m)


def setup_inputs(seed: int = 0) -> dict:
    key = jax.random.key(seed)
    ks = jax.random.split(key, 32)
    f32 = jnp.float32
    nrm = lambda i, shape, s: jax.random.normal(ks[i], shape, f32) * s
    gain = lambda i, shape: 1.0 + 0.01 * jax.random.normal(ks[i], shape, f32)
    return {
        "x_prompt": nrm(0, (BATCH, SEQ, D_MODEL), 1.0),
        "x_sample": nrm(1, (DEC_BATCH, DEC_SEQ, D_MODEL), 1.0),
        "cache_k": nrm(2, (DEPTH, DEC_BATCH, WINDOW, A_KV_HEADS, A_HEAD_DIM), 1.0),
        "cache_v": nrm(3, (DEPTH, DEC_BATCH, WINDOW, A_KV_HEADS, A_HEAD_DIM), 1.0),
        "state_pool": nrm(4, (DEPTH, DEC_BATCH, POOL_BUF, POOL_W), 1.0),
        "state_mlstm_c": nrm(5, (DEPTH, DEC_BATCH, C_HEADS, C_HEAD_DIM, C_HEAD_DIM), 0.1),
        "state_mlstm_n": nrm(6, (DEPTH, DEC_BATCH, C_HEADS, C_HEAD_DIM), 0.5),
        "state_mlstm_m": nrm(7, (DEPTH, DEC_BATCH, C_HEADS), 1.0),
        "rel_bias": nrm(8, (N_BUCKETS, A_HEADS), 0.5),
        "norm1_g": gain(9, (DEPTH, D_MODEL)),
        "w_in": nrm(10, (DEPTH, D_MODEL, IN_COLS), D_MODEL ** -0.5),
        "q_norm_g": gain(11, (DEPTH, A_HEAD_DIM)),
        "k_norm_g": gain(12, (DEPTH, A_HEAD_DIM)),
        "attn_sink": nrm(13, (DEPTH, A_HEADS), 0.5),
        "w_pool": nrm(14, (DEPTH, POOL_GROUPS, POOL_GROUP_W, POOL_GROUP_W), POOL_GROUP_W ** -0.5),
        "pool_scale": 0.5 + nrm(15, (DEPTH, POOL_W), 0.1),
        "b_igate": nrm(16, (DEPTH, C_HEADS), 0.1),
        "b_fgate": 3.0 + nrm(17, (DEPTH, C_HEADS), 0.5),
        "h_norm_g": gain(18, (DEPTH, C_HEAD_DIM)),
        "w_br_a": nrm(19, (DEPTH, A_Q_COLS, D_MODEL), A_Q_COLS ** -0.5),
        "w_br_b": nrm(20, (DEPTH, POOL_W, D_MODEL), POOL_W ** -0.5),
        "w_br_c": nrm(21, (DEPTH, C_W, D_MODEL), C_W ** -0.5),
        "w_out": nrm(22, (DEPTH, D_MODEL, D_MODEL), D_MODEL ** -0.5),
        "norm2_g": gain(23, (DEPTH, D_MODEL)),
        "peer_wq": nrm(24, (DEPTH, D_MODEL, PEER_HEADS * PEER_DK), D_MODEL ** -0.5),
        "peer_k1": nrm(25, (DEPTH, N_KEYS, PEER_DK // 2), (PEER_DK // 2) ** -0.5),
        "peer_k2": nrm(26, (DEPTH, N_KEYS, PEER_DK // 2), (PEER_DK // 2) ** -0.5),
        "peer_u": nrm(27, (DEPTH, N_EXPERTS, D_MODEL), D_MODEL ** -0.5),
        "peer_v": nrm(28, (DEPTH, N_EXPERTS, D_MODEL), PEER_TOPK ** -0.5),
    }


def reference(x_prompt, x_sample, cache_k, cache_v, state_pool, state_mlstm_c, state_mlstm_n, state_mlstm_m,
              rel_bias, norm1_g, w_in, q_norm_g, k_norm_g, attn_sink, w_pool, pool_scale, b_igate, b_fgate,
              h_norm_g, w_br_a, w_br_b, w_br_c, w_out, norm2_g, peer_wq, peer_k1, peer_k2, peer_u, peer_v):
    B = x_prompt.shape[0]
    xp, xs = x_prompt, x_sample
    new_p = [[] for _ in range(6)]
    new_s = [[] for _ in range(6)]
    for l in range(DEPTH):
        p = dict(norm1_g=norm1_g[l], w_in=w_in[l], q_norm_g=q_norm_g[l], k_norm_g=k_norm_g[l], sink=attn_sink[l],
                 w_pool=w_pool[l], pool_scale=pool_scale[l], b_igate=b_igate[l], b_fgate=b_fgate[l],
                 h_norm_g=h_norm_g[l], w_br_a=w_br_a[l], w_br_b=w_br_b[l], w_br_c=w_br_c[l], w_out=w_out[l],
                 norm2_g=norm2_g[l], peer_wq=peer_wq[l], peer_k1=peer_k1[l], peer_k2=peer_k2[l],
                 peer_u=peer_u[l], peer_v=peer_v[l])
        st_p = dict(pool=jnp.zeros((B, POOL_BUF, POOL_W), xp.dtype),
                    c=jnp.zeros((B, C_HEADS, C_HEAD_DIM, C_HEAD_DIM), jnp.float32),
                    n=jnp.zeros((B, C_HEADS, C_HEAD_DIM), jnp.float32),
                    m=jnp.zeros((B, C_HEADS), jnp.float32))
        st_s = dict(k=cache_k[l], v=cache_v[l], pool=state_pool[l], c=state_mlstm_c[l], n=state_mlstm_n[l],
                    m=state_mlstm_m[l])
        xp, sp = _layer(xp, True, st_p, p, rel_bias)
        xs, ss = _layer(xs, False, st_s, p, rel_bias)
        for i in range(6):
            new_p[i].append(sp[i])
            new_s[i].append(ss[i])
    kp, vp, poolp, cp, np_, mp = [jnp.stack(a) for a in new_p]
    ks, vs, pools, cs, ns, ms = [jnp.stack(a) for a in new_s]
    return (xp, xs, kp, vp, poolp, cp, np_, mp, ks, vs, pools, cs, ns, ms)
```

```python
import functools
import math

import jax
import jax.numpy as jnp
import numpy as np
from jax import lax
from jax.experimental import pallas as pl
from jax.experimental.pallas import tpu as pltpu

f32 = jnp.float32
bf16 = jnp.bfloat16
i32 = jnp.int32

D_MODEL = 1024
A_HEADS = 8
A_KV_HEADS = 2
A_HEAD_DIM = 64
A_GROUP = A_HEADS // A_KV_HEADS
WINDOW = 128
N_BUCKETS = 32
MAX_DISTANCE = 128
A_Q_COLS = A_HEADS * A_HEAD_DIM
A_KV_COLS = A_KV_HEADS * A_HEAD_DIM
POOL_WINDOWS = (2, 4, 8, 16)
POOL_GROUP_W = 128
POOL_W = 512
POOL_BUF = 15
C_HEADS = 4
C_HEAD_DIM = 128
C_W = C_HEADS * C_HEAD_DIM
C_CHUNK = 128
PAST_LEN = 16384
PEER_HEADS = 8
PEER_DK = 256
N_KEYS = 128
N_EXPERTS = N_KEYS * N_KEYS
PEER_TOPK = 16
PEER_PICKS = PEER_HEADS * PEER_TOPK
EPS = 1e-6
NEG_INF = float("-inf")

LANES = 128
ROW_CHUNKS = D_MODEL // LANES
HALF_CHUNKS = ROW_CHUNKS // 2
MIB = 1024 * 1024

PROJ_SPLITS = (A_Q_COLS + 2 * A_KV_COLS, POOL_W, 4 * C_W, LANES, 3 * D_MODEL)

STAIR_ROWS = tuple(PEER_TOPK // (a + 1) for a in range(PEER_TOPK))
STAIR_N = sum(STAIR_ROWS)
STAIR_PAD = (-STAIR_N) % 8


def _cparams(sem, vmem_mib):
    return pltpu.CompilerParams(dimension_semantics=sem, vmem_limit_bytes=vmem_mib * MIB)


def _resident(shape):
    nd = len(shape)
    return pl.BlockSpec(shape, lambda *_: (0,) * nd, pipeline_mode=pl.Buffered(1))


def _dot(a, b):
    return jnp.dot(a, b, preferred_element_type=f32)


def _dot_nt(a, b):
    return lax.dot_general(a, b, (((1,), (1,)), ((), ())), preferred_element_type=f32)


def _dot_tn(a, b):
    return lax.dot_general(a, b, (((0,), (0,)), ((), ())), preferred_element_type=f32)


def _split_bf16(x):
    hi = x.astype(bf16)
    lo = (x - hi.astype(f32)).astype(bf16)
    return hi, lo


def _sigmoid(x):
    return 1.0 / (1.0 + jnp.exp(-x))


def _log_sigmoid(x):
    return jnp.minimum(x, 0.0) - jnp.log(1.0 + jnp.exp(-jnp.abs(x)))


def _rms_rows(x, g):
    ms = jnp.mean(x * x, axis=-1, keepdims=True)
    return x * lax.rsqrt(ms + EPS) * g


def _headnorm(x, blockdiag, g, width):
    hi, lo = _split_bf16(x * x)
    ss = _dot(hi, blockdiag) + _dot(lo, blockdiag)
    return x * lax.rsqrt(ss * (1.0 / width) + EPS) * g


def _proj_kernel(x_ref, g_ref, w_ref, *o_refs):
    xn = _rms_rows(x_ref[...], g_ref[...]).astype(bf16)
    off = 0
    for o_ref, width in zip(o_refs, PROJ_SPLITS):
        o_ref[...] = _dot(xn, w_ref[:, off:off + width])
        off += width


def _proj(x2d, g, w_all):
    m = x2d.shape[0]
    tm = min(256, m)
    total = sum(PROJ_SPLITS)
    return pl.pallas_call(
        _proj_kernel,
        grid=(m // tm,),
        in_specs=[pl.BlockSpec((tm, D_MODEL), lambda i: (i, 0)),
                  _resident((1, D_MODEL)),
                  _resident((D_MODEL, total))],
        out_specs=[pl.BlockSpec((tm, w), lambda i: (i, 0)) for w in PROJ_SPLITS],
        out_shape=[jax.ShapeDtypeStruct((m, w), f32) for w in PROJ_SPLITS],
        compiler_params=_cparams(("parallel",), 48),
        name="proj",
    )(x2d, g.reshape(1, D_MODEL), w_all)


def _swa_kernel(sink_ref, zc_ref, kp_ref, vp_ref, bias_ref, qg_ref, kg_ref, bd_ref, ya_ref, kn_ref):
    i = pl.program_id(1)
    zc = zc_ref[0]
    bd = bd_ref[...]
    bdk = bd[0:A_KV_COLS, 0:A_KV_COLS]
    qn = _headnorm(zc[:, 0:A_Q_COLS], bd, qg_ref[...], A_HEAD_DIM)
    kcn = _headnorm(zc[:, A_Q_COLS:A_Q_COLS + A_KV_COLS], bdk, kg_ref[...], A_HEAD_DIM)
    kpn = _headnorm(kp_ref[0], bdk, kg_ref[...], A_HEAD_DIM)
    kn_ref[0] = kcn
    vc = zc[:, A_Q_COLS + A_KV_COLS:A_Q_COLS + 2 * A_KV_COLS]
    kcat = jnp.concatenate([kpn, kcn], axis=0).astype(bf16)
    vcat = jnp.concatenate([vp_ref[0], vc], axis=0).astype(bf16)
    qb = qn.astype(bf16)
    col = lax.broadcasted_iota(i32, (WINDOW, 2 * WINDOW), 1)
    has_prev = jnp.logical_or(col >= WINDOW, i > 0)
    outs = []
    for h in range(A_HEADS):
        kv = h // A_GROUP
        ks = slice(kv * A_HEAD_DIM, (kv + 1) * A_HEAD_DIM)
        logits = _dot_nt(qb[:, h * A_HEAD_DIM:(h + 1) * A_HEAD_DIM], kcat[:, ks]) * (A_HEAD_DIM ** -0.5)
        logits = jnp.where(has_prev, logits + bias_ref[h], NEG_INF)
        sink = sink_ref[h]
        mx = jnp.maximum(jnp.max(logits, axis=-1, keepdims=True), sink)
        p = jnp.exp(logits - mx)
        p = p / (jnp.sum(p, axis=-1, keepdims=True) + jnp.exp(sink - mx))
        outs.append(_dot(p.astype(bf16), vcat[:, ks]))
    ya_ref[0] = jnp.concatenate(outs, axis=-1)


def _swa_prompt(z_a, bias_tab, sink, qg, kg, bd, n, t):
    nb = t // WINDOW
    z3 = z_a.reshape(n, t, PROJ_SPLITS[0])
    kblk = A_Q_COLS // A_KV_COLS
    prev = lambda b, i: (b, jnp.maximum(i - 1, 0), kblk)
    prev_v = lambda b, i: (b, jnp.maximum(i - 1, 0), kblk + 1)
    ya, kn = pl.pallas_call(
        _swa_kernel,
        grid=(n, nb),
        in_specs=[pl.BlockSpec(memory_space=pltpu.SMEM),
                  pl.BlockSpec((1, WINDOW, PROJ_SPLITS[0]), lambda b, i: (b, i, 0)),
                  pl.BlockSpec((1, WINDOW, A_KV_COLS), prev),
                  pl.BlockSpec((1, WINDOW, A_KV_COLS), prev_v),
                  _resident((A_HEADS, WINDOW, 2 * WINDOW)),
                  _resident((1, A_Q_COLS)),
                  _resident((1, A_KV_COLS)),
                  _resident((A_Q_COLS, A_Q_COLS))],
        out_specs=[pl.BlockSpec((1, WINDOW, A_Q_COLS), lambda b, i: (b, i, 0)),
                   pl.BlockSpec((1, WINDOW, A_KV_COLS), lambda b, i: (b, i, 0))],
        out_shape=[jax.ShapeDtypeStruct((n, t, A_Q_COLS), f32),
                   jax.ShapeDtypeStruct((n, t, A_KV_COLS), f32)],
        compiler_params=_cparams(("parallel", "arbitrary"), 32),
        name="swa_prompt",
    )(sink, z3, z3, z3, bias_tab, qg, kg, bd)
    return ya, kn


def _swa_sample_kernel(q_ref, kn_ref, vn_ref, ck_ref, cv_ref, bias_ref, sink_ref, qg_ref, kg_ref, bd_ref,
                       ya_ref, ko_ref, vo_ref):
    q = q_ref[...]
    qn = q * lax.rsqrt(jnp.mean(q * q, axis=-1, keepdims=True) + EPS) * qg_ref[...]
    kn = _headnorm(kn_ref[...], bd_ref[...], kg_ref[...], A_HEAD_DIM)
    rowi = lax.broadcasted_iota(i32, ck_ref.shape, 1)
    newest = rowi == WINDOW - 1
    k_all = jnp.where(newest, kn[:, None, :], pltpu.roll(ck_ref[...], WINDOW - 1, 1))
    v_all = jnp.where(newest, vn_ref[...][:, None, :], pltpu.roll(cv_ref[...], WINDOW - 1, 1))
    ko_ref[...] = k_all
    vo_ref[...] = v_all
    outs = []
    for kv in range(A_KV_HEADS):
        ks = slice(kv * A_HEAD_DIM, (kv + 1) * A_HEAD_DIM)
        hs = slice(kv * A_GROUP, (kv + 1) * A_GROUP)
        kh = k_all[:, :, ks].astype(bf16)
        vh = v_all[:, :, ks].astype(bf16)
        qh = qn[:, hs, :].astype(bf16)
        logits = jnp.einsum('bgd,bsd->bgs', qh, kh, preferred_element_type=f32) * (A_HEAD_DIM ** -0.5)
        logits = logits + bias_ref[hs, :][None]
        sink = sink_ref[hs, :][None]
        mx = jnp.maximum(jnp.max(logits, axis=-1, keepdims=True), sink)
        p = jnp.exp(logits - mx)
        p = p / (jnp.sum(p, axis=-1, keepdims=True) + jnp.exp(sink - mx))
        outs.append(jnp.einsum('bgs,bsd->bgd', p.astype(bf16), vh, preferred_element_type=f32))
    ya_ref[...] = jnp.concatenate(outs, axis=1)


def _swa_sample(z_a, cache_k, cache_v, bias_s, sink, qg64, kg, bdk):
    n = z_a.shape[0]
    bn = 8
    q3 = z_a[:, 0:A_Q_COLS].reshape(n, A_HEADS, A_HEAD_DIM)
    k_new = z_a[:, A_Q_COLS:A_Q_COLS + A_KV_COLS]
    v_new = z_a[:, A_Q_COLS + A_KV_COLS:]
    ck = cache_k.reshape(n, WINDOW, A_KV_COLS)
    cv = cache_v.reshape(n, WINDOW, A_KV_COLS)
    row2 = pl.BlockSpec((bn, A_KV_COLS), lambda i: (i, 0))
    cache = pl.BlockSpec((bn, WINDOW, A_KV_COLS), lambda i: (i, 0, 0))
    q_spec = pl.BlockSpec((bn, A_HEADS, A_HEAD_DIM), lambda i: (i, 0, 0))
    ya, ko, vo = pl.pallas_call(
        _swa_sample_kernel,
        grid=(n // bn,),
        in_specs=[q_spec, row2, row2, cache, cache,
                  _resident((A_HEADS, WINDOW)), _resident((A_HEADS, 1)),
                  _resident((1, A_HEAD_DIM)), _resident((1, A_KV_COLS)), _resident((A_KV_COLS, A_KV_COLS))],
        out_specs=[q_spec, cache, cache],
        out_shape=[jax.ShapeDtypeStruct((n, A_HEADS, A_HEAD_DIM), f32),
                   jax.ShapeDtypeStruct((n, WINDOW, A_KV_COLS), f32),
                   jax.ShapeDtypeStruct((n, WINDOW, A_KV_COLS), f32)],
        compiler_params=_cparams(("parallel",), 32),
        name="swa_sample",
    )(q3, k_new, v_new, ck, cv, bias_s, sink.reshape(A_HEADS, 1), qg64, kg, bdk)
    shape = (n, WINDOW, A_KV_HEADS, A_HEAD_DIM)
    return ya.reshape(n, A_Q_COLS), ko.reshape(shape), vo.reshape(shape)


def _pool_kernel(x_ref, w_ref, scale_ref, y_ref, buf_ref):
    j = pl.program_id(1)
    bt = x_ref.shape[1]
    halo = POOL_BUF + 1

    @pl.when(j == 0)
    def _():
        buf_ref[0:halo, :] = jnp.zeros((halo, POOL_W), f32)

    x = x_ref[0]
    buf_ref[halo:halo + bt, :] = x
    pos = j * bt + lax.broadcasted_iota(i32, (bt, 1), 0)
    outs = []
    for g, w in enumerate(POOL_WINDOWS):
        cs = slice(g * POOL_GROUP_W, (g + 1) * POOL_GROUP_W)
        xg = x[:, cs]
        s = xg
        for k in range(1, w):
            s = s + buf_ref[halo - k:halo - k + bt, cs]
        cnt = jnp.minimum(pos + 1, w).astype(f32)
        pooled = s / cnt - xg
        outs.append(_dot(pooled.astype(bf16), w_ref[g]))
    y_ref[0] = jnp.concatenate(outs, axis=-1) * scale_ref[...]
    buf_ref[0:halo, :] = x[bt - halo:bt, :]


def _pool_prompt(z_pool, w_pool, scale, n, t):
    bt = min(512, t)
    z3 = z_pool.reshape(n, t, POOL_W)
    spec = pl.BlockSpec((1, bt, POOL_W), lambda b, j: (b, j, 0))
    return pl.pallas_call(
        _pool_kernel,
        grid=(n, t // bt),
        in_specs=[spec, _resident((4, POOL_GROUP_W, POOL_GROUP_W)), _resident((1, POOL_W))],
        out_specs=spec,
        out_shape=jax.ShapeDtypeStruct((n, t, POOL_W), f32),
        scratch_shapes=[pltpu.VMEM((bt + POOL_BUF + 1, POOL_W), f32)],
        compiler_params=_cparams(("parallel", "arbitrary"), 32),
        name="pool_prompt",
    )(z3, w_pool, scale)


def _pool_sample_kernel(x_ref, st_ref, w_ref, scale_ref, y_ref):
    x = x_ref[...]
    outs = []
    for g, w in enumerate(POOL_WINDOWS):
        cs = slice(g * POOL_GROUP_W, (g + 1) * POOL_GROUP_W)
        xg = x[:, cs]
        s = xg
        for k in range(1, w):
            s = s + st_ref[:, POOL_BUF - k, cs]
        cnt = float(min(PAST_LEN + 1, w))
        pooled = s / cnt - xg
        outs.append(_dot(pooled.astype(bf16), w_ref[g]))
    y_ref[...] = jnp.concatenate(outs, axis=-1) * scale_ref[...]


def _pool_sample(z_pool, state, w_pool, scale):
    n = z_pool.shape[0]
    bn = 8
    return pl.pallas_call(
        _pool_sample_kernel,
        grid=(n // bn,),
        in_specs=[pl.BlockSpec((bn, POOL_W), lambda i: (i, 0)),
                  pl.BlockSpec((bn, POOL_BUF, POOL_W), lambda i: (i, 0, 0)),
                  _resident((4, POOL_GROUP_W, POOL_GROUP_W)), _resident((1, POOL_W))],
        out_specs=pl.BlockSpec((bn, POOL_W), lambda i: (i, 0)),
        out_shape=jax.ShapeDtypeStruct((n, POOL_W), f32),
        compiler_params=_cparams(("parallel",), 32),
        name="pool_sample",
    )(z_pool, state, w_pool, scale)


def _mlstm_kernel(zc_ref, zif_ref, gb_ref, tri_ref, h_ref, c_out, n_out, m_out, c_s, n_s, m_s):
    j = pl.program_id(1)
    L = C_CHUNK

    @pl.when(j == 0)
    def _():
        c_s[...] = jnp.zeros(c_s.shape, f32)
        n_s[...] = jnp.zeros(n_s.shape, f32)
        m_s[...] = jnp.zeros(m_s.shape, f32)

    zc = zc_ref[0]
    gates = zif_ref[0] + gb_ref[...]
    lf_hi, lf_lo = _split_bf16(_log_sigmoid(gates))
    bcum = _dot(tri_ref[...], lf_hi) + _dot(tri_ref[...], lf_lo)
    bcum_t = bcum.T
    gates_t = gates.T
    row = lax.broadcasted_iota(i32, (L, L), 0)
    col = lax.broadcasted_iota(i32, (L, L), 1)
    causal = col <= row
    for h in range(C_HEADS):
        hs = slice(h * C_HEAD_DIM, (h + 1) * C_HEAD_DIM)
        q = zc[:, hs]
        k = zc[:, C_W + h * C_HEAD_DIM:C_W + (h + 1) * C_HEAD_DIM] * (C_HEAD_DIM ** -0.5)
        v = zc[:, 2 * C_W + h * C_HEAD_DIM:2 * C_W + (h + 1) * C_HEAD_DIM]
        b_col = bcum[:, C_HEADS + h:C_HEADS + h + 1]
        b_row = bcum_t[C_HEADS + h:C_HEADS + h + 1, :]
        ig_col = gates[:, h:h + 1]
        ig_row = gates_t[h:h + 1, :]
        m_prev = m_s[h:h + 1, 0:1]
        dmat = jnp.where(causal, b_col - b_row + ig_row, NEG_INF)
        inter = b_col + m_prev
        m_t = jnp.maximum(inter, jnp.max(dmat, axis=-1, keepdims=True))
        qb, kb, vb = q.astype(bf16), k.astype(bf16), v.astype(bf16)
        s = _dot_nt(qb, kb) * jnp.exp(dmat - m_t)
        w_inter = jnp.exp(inter - m_t)
        c_prev = c_s[h]
        n_prev = n_s[h:h + 1, :]
        num = _dot(s.astype(bf16), vb) + w_inter * _dot_nt(qb, c_prev.astype(bf16))
        den = jnp.sum(s, axis=-1, keepdims=True) + w_inter * jnp.sum(q * n_prev, axis=-1, keepdims=True)
        h_ref[0, :, hs] = num / jnp.maximum(jnp.abs(den), jnp.exp(-m_t))
        m_new = m_t[L - 1:L, :]
        b_last = b_col[L - 1:L, :]
        w_end = jnp.exp(b_last - b_col + ig_col - m_new)
        decay = jnp.exp(b_last + m_prev - m_new)
        c_s[h] = decay * c_prev + _dot_tn((v * w_end).astype(bf16), kb)
        n_s[h:h + 1, :] = decay * n_prev + jnp.sum(k * w_end, axis=0, keepdims=True)
        m_s[h:h + 1, :] = jnp.broadcast_to(m_new, (1, LANES))

    @pl.when(j == pl.num_programs(1) - 1)
    def _():
        c_out[0] = c_s[...]
        n_out[0] = n_s[0:C_HEADS, :]
        m_out[0] = m_s[...]


def _mlstm_prompt(z_c, z_if, gate_bias, tri, n, t):
    nc = t // C_CHUNK
    zc3 = z_c.reshape(n, t, 4 * C_W)
    zif3 = z_if.reshape(n, t, LANES)
    return pl.pallas_call(
        _mlstm_kernel,
        grid=(n, nc),
        in_specs=[pl.BlockSpec((1, C_CHUNK, 3 * C_W), lambda b, j: (b, j, 0)),
                  pl.BlockSpec((1, C_CHUNK, LANES), lambda b, j: (b, j, 0)),
                  _resident((1, LANES)),
                  _resident((C_CHUNK, C_CHUNK))],
        out_specs=[pl.BlockSpec((1, C_CHUNK, C_W), lambda b, j: (b, j, 0)),
                   pl.BlockSpec((1, C_HEADS, C_HEAD_DIM, C_HEAD_DIM), lambda b, j: (b, 0, 0, 0)),
                   pl.BlockSpec((1, C_HEADS, C_HEAD_DIM), lambda b, j: (b, 0, 0)),
                   pl.BlockSpec((1, 8, LANES), lambda b, j: (b, 0, 0))],
        out_shape=[jax.ShapeDtypeStruct((n, t, C_W), f32),
                   jax.ShapeDtypeStruct((n, C_HEADS, C_HEAD_DIM, C_HEAD_DIM), f32),
                   jax.ShapeDtypeStruct((n, C_HEADS, C_HEAD_DIM), f32),
                   jax.ShapeDtypeStruct((n, 8, LANES), f32)],
        scratch_shapes=[pltpu.VMEM((C_HEADS, C_HEAD_DIM, C_HEAD_DIM), f32),
                        pltpu.VMEM((8, LANES), f32),
                        pltpu.VMEM((8, LANES), f32)],
        compiler_params=_cparams(("parallel", "arbitrary"), 32),
        name="mlstm_prompt",
    )(zc3, zif3, gate_bias, tri)


def _mlstm_sample_kernel(zc_ref, zif_ref, gb_ref, c_ref, n_ref, m_ref, h_ref, co_ref, no_ref, mo_ref):
    bn = zc_ref.shape[0]
    zc = zc_ref[...]
    gates = zif_ref[...] + gb_ref[...]
    lfs = _log_sigmoid(gates)
    lane = lax.broadcasted_iota(i32, (C_HEAD_DIM, LANES), 1)
    pad = jnp.zeros((LANES - bn, C_HEAD_DIM), f32)
    for h in range(C_HEADS):
        hs = slice(h * C_HEAD_DIM, (h + 1) * C_HEAD_DIM)
        q = zc[:, hs]
        k = zc[:, C_W + h * C_HEAD_DIM:C_W + (h + 1) * C_HEAD_DIM] * (C_HEAD_DIM ** -0.5)
        v = zc[:, 2 * C_W + h * C_HEAD_DIM:2 * C_W + (h + 1) * C_HEAD_DIM]
        ig = gates[:, h:h + 1]
        lf = lfs[:, C_HEADS + h:C_HEADS + h + 1]
        m_prev = m_ref[:, h:h + 1]
        n_prev = n_ref[:, h, :]
        inter = lf + m_prev
        m_t = jnp.maximum(inter, ig)
        s = jnp.sum(q * k, axis=-1, keepdims=True) * jnp.exp(ig - m_t)
        w_inter = jnp.exp(inter - m_t)
        w_end = jnp.exp(ig - m_t)
        cq_t = jnp.zeros((C_HEAD_DIM, LANES), f32)
        for b in range(bn):
            colsum = jnp.sum(c_ref[b, h] * q[b:b + 1, :], axis=-1, keepdims=True)
            cq_t = jnp.where(lane == b, colsum, cq_t)
        cq = cq_t.T[0:bn, :]
        num = s * v + w_inter * cq
        den = s + w_inter * jnp.sum(n_prev * q, axis=-1, keepdims=True)
        h_ref[:, hs] = num / jnp.maximum(jnp.abs(den), jnp.exp(-m_t))
        vw_t = jnp.concatenate([v * w_end, pad], axis=0).T
        for b in range(bn):
            co_ref[b, h] = w_inter[b:b + 1, :] * c_ref[b, h] + vw_t[:, b:b + 1] * k[b:b + 1, :]
        no_ref[:, h, :] = w_inter * n_prev + w_end * k
        mo_ref[:, h:h + 1] = m_t


def _mlstm_sample(z_c, z_if, gate_bias, c0, n0, m0):
    n = z_c.shape[0]
    bn = 8
    c_spec = pl.BlockSpec((bn, C_HEADS, C_HEAD_DIM, C_HEAD_DIM), lambda i: (i, 0, 0, 0))
    n_spec = pl.BlockSpec((bn, C_HEADS, C_HEAD_DIM), lambda i: (i, 0, 0))
    m_spec = pl.BlockSpec((bn, C_HEADS), lambda i: (i, 0))
    return pl.pallas_call(
        _mlstm_sample_kernel,
        grid=(n // bn,),
        in_specs=[pl.BlockSpec((bn, 3 * C_W), lambda i: (i, 0)),
                  pl.BlockSpec((bn, LANES), lambda i: (i, 0)),
                  _resident((1, LANES)), c_spec, n_spec, m_spec],
        out_specs=[pl.BlockSpec((bn, C_W), lambda i: (i, 0)), c_spec, n_spec, m_spec],
        out_shape=[jax.ShapeDtypeStruct((n, C_W), f32),
                   jax.ShapeDtypeStruct(c0.shape, f32),
                   jax.ShapeDtypeStruct(n0.shape, f32),
                   jax.ShapeDtypeStruct(m0.shape, f32)],
        compiler_params=_cparams(("parallel",), 32),
        name="mlstm_sample",
    )(z_c, z_if, gate_bias, c0, n0, m0)


def _merge_kernel(x_ref, ya_ref, yb_ref, h_ref, co_ref, zg_ref, hg_ref, wa_ref, wb_ref, wc_ref, wo_ref, o_ref):
    h = h_ref[...]
    parts = []
    for hh in range(C_HEADS):
        hs = h[:, hh * C_HEAD_DIM:(hh + 1) * C_HEAD_DIM]
        parts.append(hs * lax.rsqrt(jnp.mean(hs * hs, axis=-1, keepdims=True) + EPS))
    yc = jnp.concatenate(parts, axis=-1) * hg_ref[...] * _sigmoid(co_ref[...])
    zg = zg_ref[...]
    merged = (_sigmoid(zg[:, 0:D_MODEL]) * _dot(ya_ref[...].astype(bf16), wa_ref[...])
              + _sigmoid(zg[:, D_MODEL:2 * D_MODEL]) * _dot(yb_ref[...].astype(bf16), wb_ref[...])
              + _sigmoid(zg[:, 2 * D_MODEL:]) * _dot(yc.astype(bf16), wc_ref[...]))
    o_ref[...] = x_ref[...] + _dot(merged.astype(bf16), wo_ref[...])


def _merge(x2d, ya, yb, hc, z_c, z_g, hg, wa, wb, wc, wo):
    m = x2d.shape[0]
    tb = min(256, m)
    row = lambda w: pl.BlockSpec((tb, w), lambda i: (i, 0))
    return pl.pallas_call(
        _merge_kernel,
        grid=(m // tb,),
        in_specs=[row(D_MODEL), row(A_Q_COLS), row(POOL_W), row(C_W),
                  pl.BlockSpec((tb, C_W), lambda i: (i, 3)),
                  row(3 * D_MODEL), _resident((1, C_W)),
                  _resident((A_Q_COLS, D_MODEL)), _resident((POOL_W, D_MODEL)), _resident((C_W, D_MODEL)),
                  _resident((D_MODEL, D_MODEL))],
        out_specs=row(D_MODEL),
        out_shape=jax.ShapeDtypeStruct((m, D_MODEL), f32),
        compiler_params=_cparams(("parallel",), 48),
        name="merge",
    )(x2d, ya, yb, hc, z_c, z_g, hg, wa, wb, wc, wo)


def _topk_rows(s, k):
    nrows = s.shape[0]
    riota = lax.broadcasted_iota(i32, s.shape, 0)
    vals, idxs = [], []
    for _ in range(k):
        m = jnp.max(s, axis=0, keepdims=True)
        i = jnp.min(jnp.where(s == m, riota, nrows), axis=0, keepdims=True)
        vals.append(m)
        idxs.append(i)
        s = jnp.where(riota == i, NEG_INF, s)
    return vals, idxs


def _route_kernel(x_ref, g_ref, wq_ref, k1_ref, k2_ref, idx_ref, gate_ref, xn_ref):
    tb = x_ref.shape[0]
    xn = _rms_rows(x_ref[...], g_ref[...])
    xn_ref[...] = xn
    qb = _dot(xn.astype(bf16), wq_ref[...]).astype(bf16)
    half = PEER_DK // 2
    exp_rows, gate_rows = [], []
    for h in range(PEER_HEADS):
        s1 = _dot_nt(k1_ref[...], qb[:, h * PEER_DK:h * PEER_DK + half])
        s2 = _dot_nt(k2_ref[...], qb[:, h * PEER_DK + half:(h + 1) * PEER_DK])
        v1, i1 = _topk_rows(s1, PEER_TOPK)
        v2, i2 = _topk_rows(s2, PEER_TOPK)
        v2s = jnp.concatenate(v2, axis=0)
        i2s = jnp.concatenate(i2, axis=0)
        cv, ce = [], []
        for a, nb in enumerate(STAIR_ROWS):
            cv.append(v1[a] + v2s[0:nb])
            ce.append(i1[a] * N_KEYS + i2s[0:nb])
        if STAIR_PAD:
            cv.append(jnp.full((STAIR_PAD, tb), NEG_INF, f32))
            ce.append(jnp.zeros((STAIR_PAD, tb), i32))
        cand = jnp.concatenate(cv, axis=0)
        cexp = jnp.concatenate(ce, axis=0)
        riota = lax.broadcasted_iota(i32, cand.shape, 0)
        tv, ti = _topk_rows(cand, PEER_TOPK)
        te = [jnp.max(jnp.where(riota == i, cexp, -1), axis=0, keepdims=True) for i in ti]
        tvs = jnp.concatenate(tv, axis=0)
        ex = jnp.exp(tvs - tv[0])
        gate_rows.append(ex / jnp.sum(ex, axis=0, keepdims=True))
        exp_rows.append(jnp.concatenate(te, axis=0))
    idx_ref[...] = jnp.concatenate(exp_rows, axis=0).T
    gate_ref[...] = jnp.concatenate(gate_rows, axis=0).T


def _route(x2d, g, wq, k1, k2):
    m = x2d.shape[0]
    tb = min(256, m)
    return pl.pallas_call(
        _route_kernel,
        grid=(m // tb,),
        in_specs=[pl.BlockSpec((tb, D_MODEL), lambda i: (i, 0)),
                  _resident((1, D_MODEL)),
                  _resident((D_MODEL, PEER_HEADS * PEER_DK)),
                  _resident((N_KEYS, PEER_DK // 2)), _resident((N_KEYS, PEER_DK // 2))],
        out_specs=[pl.BlockSpec((tb, PEER_PICKS), lambda i: (i, 0)),
                   pl.BlockSpec((tb, PEER_PICKS), lambda i: (i, 0)),
                   pl.BlockSpec((tb, D_MODEL), lambda i: (i, 0))],
        out_shape=[jax.ShapeDtypeStruct((m, PEER_PICKS), i32),
                   jax.ShapeDtypeStruct((m, PEER_PICKS), f32),
                   jax.ShapeDtypeStruct((m, D_MODEL), f32)],
        compiler_params=_cparams(("parallel",), 48),
        name="peer_route",
    )(x2d, g.reshape(1, D_MODEL), wq, k1, k2)


def _pack_table(w):
    bits = lax.bitcast_convert_type(w.astype(bf16), jnp.uint16).astype(jnp.uint32)
    packed = bits[:, :D_MODEL // 2] | (bits[:, D_MODEL // 2:] << 16)
    return lax.bitcast_convert_type(packed, i32).reshape(w.shape[0], HALF_CHUNKS, LANES)


def _unpack(row):
    lo = pltpu.bitcast(row << 16, f32)
    hi = pltpu.bitcast(row & jnp.int32(-65536), f32)
    return lo, hi


def _peer_u_kernel(idx_ref, x_ref, gate_ref, sel_ref, tab_ref, w_ref, p_ref, act_ref):
    tb = x_ref.shape[0]
    ones = jnp.ones((8, LANES), bf16)

    def token(t, carry):
        xt = x_ref[t]
        x_lo = xt[0:HALF_CHUNKS]
        x_hi = xt[HALF_CHUNKS:ROW_CHUNKS]
        base = t * PEER_PICKS
        for j in range(PEER_PICKS):
            lo, hi = _unpack(tab_ref[idx_ref[base + j]])
            p_ref[HALF_CHUNKS * j:HALF_CHUNKS * (j + 1), :] = lo * x_lo + hi * x_hi
        part = _dot(sel_ref[...], p_ref[...].astype(bf16))
        act_ref[t] = _dot_nt(ones, part.astype(bf16))[0:1, :]
        return carry

    lax.fori_loop(0, tb, token, 0)
    a = act_ref[...]
    gelu = 0.5 * a * (1.0 + lax.erf(a * (2.0 ** -0.5)))
    w_ref[...] = gate_ref[...] * gelu


def _peer_u(idx_flat, xn3, gates3, sel, tab):
    m = xn3.shape[0]
    tb = min(64, m)
    return pl.pallas_call(
        _peer_u_kernel,
        grid=(m // tb,),
        in_specs=[pl.BlockSpec((tb * PEER_PICKS,), lambda i: (i,), memory_space=pltpu.SMEM),
                  pl.BlockSpec((tb, ROW_CHUNKS, LANES), lambda i: (i, 0, 0)),
                  pl.BlockSpec((tb, 1, PEER_PICKS), lambda i: (i, 0, 0)),
                  _resident((PEER_PICKS, HALF_CHUNKS * PEER_PICKS)),
                  _resident((N_EXPERTS, HALF_CHUNKS, LANES))],
        out_specs=pl.BlockSpec((tb, 1, PEER_PICKS), lambda i: (i, 0, 0)),
        out_shape=jax.ShapeDtypeStruct((m, 1, PEER_PICKS), f32),
        scratch_shapes=[pltpu.VMEM((HALF_CHUNKS * PEER_PICKS, LANES), f32),
                        pltpu.VMEM((tb, 1, PEER_PICKS), f32)],
        compiler_params=_cparams(("parallel",), 44),
        name="peer_u",
    )(idx_flat, xn3, gates3, sel, tab)


def _peer_v_kernel(idx_ref, w_ref, x_ref, tab_ref, o_ref):
    tb = x_ref.shape[0]
    nacc = 2

    def token(t, carry):
        base = t * PEER_PICKS
        acc_lo = [jnp.zeros((HALF_CHUNKS, LANES), f32) for _ in range(nacc)]
        acc_hi = [jnp.zeros((HALF_CHUNKS, LANES), f32) for _ in range(nacc)]
        for j in range(PEER_PICKS):
            lo, hi = _unpack(tab_ref[idx_ref[base + j]])
            w = w_ref[base + j]
            acc_lo[j % nacc] = acc_lo[j % nacc] + w * lo
            acc_hi[j % nacc] = acc_hi[j % nacc] + w * hi
        y = jnp.concatenate([sum(acc_lo[1:], acc_lo[0]), sum(acc_hi[1:], acc_hi[0])], axis=0)
        o_ref[t] = x_ref[t] + y
        return carry

    lax.fori_loop(0, tb, token, 0)


def _peer_v(idx_flat, w_flat, x3, tab):
    m = x3.shape[0]
    tb = min(64, m)
    return pl.pallas_call(
        _peer_v_kernel,
        grid=(m // tb,),
        in_specs=[pl.BlockSpec((tb * PEER_PICKS,), lambda i: (i,), memory_space=pltpu.SMEM),
                  pl.BlockSpec((tb * PEER_PICKS,), lambda i: (i,), memory_space=pltpu.SMEM),
                  pl.BlockSpec((tb, ROW_CHUNKS, LANES), lambda i: (i, 0, 0)),
                  _resident((N_EXPERTS, HALF_CHUNKS, LANES))],
        out_specs=pl.BlockSpec((tb, ROW_CHUNKS, LANES), lambda i: (i, 0, 0)),
        out_shape=jax.ShapeDtypeStruct((m, ROW_CHUNKS, LANES), f32),
        compiler_params=_cparams(("parallel",), 44),
        name="peer_v",
    )(idx_flat, w_flat, x3, tab)


def _peer(x2d, p):
    m = x2d.shape[0]
    idx, gates, xn = _route(x2d, p["norm2_g"], p["wq"], p["k1"], p["k2"])
    idx_flat = idx.reshape(m * PEER_PICKS)
    w3 = _peer_u(idx_flat, xn.reshape(m, ROW_CHUNKS, LANES), gates.reshape(m, 1, PEER_PICKS), p["sel"], p["u_tab"])
    y3 = _peer_v(idx_flat, w3.reshape(m * PEER_PICKS), x2d.reshape(m, ROW_CHUNKS, LANES), p["v_tab"])
    return y3.reshape(m, D_MODEL)


def _rel_bucket(dist):
    max_exact = N_BUCKETS // 2
    d = jnp.maximum(dist, 0)
    df = jnp.maximum(d, 1).astype(f32)
    large = max_exact + (jnp.log(df / max_exact) / math.log(MAX_DISTANCE / max_exact)
                         * (N_BUCKETS - max_exact)).astype(i32)
    large = jnp.minimum(large, N_BUCKETS - 1)
    return jnp.where(d < max_exact, d, large)


def _bias_tables(rel_bias):
    dist = jnp.arange(WINDOW)[:, None] + WINDOW - jnp.arange(2 * WINDOW)[None, :]
    bias = jnp.moveaxis(rel_bias[_rel_bucket(dist)].astype(f32), -1, 0)
    bias_p = jnp.where(((dist >= 0) & (dist < WINDOW))[None], bias, NEG_INF)
    dist_s = WINDOW - 1 - jnp.arange(WINDOW)
    bias_s = rel_bias[_rel_bucket(dist_s)].astype(f32).T
    return bias_p, bias_s


def _layer_params(l, a):
    w = a["w_in"][l]
    o_pin = A_Q_COLS + 2 * A_KV_COLS
    o_c = o_pin + POOL_W
    o_if = o_c + 3 * C_W
    o_co = o_if + 2 * C_HEADS
    o_g = o_co + C_W
    w_if = jnp.pad(w[:, o_if:o_co], ((0, 0), (0, LANES - 2 * C_HEADS)))
    w_all = jnp.concatenate([w[:, 0:o_pin], w[:, o_pin:o_c], w[:, o_c:o_if], w[:, o_co:o_g], w_if, w[:, o_g:]],
                            axis=1).astype(bf16)
    gate_bias = jnp.pad(jnp.concatenate([a["b_igate"][l], a["b_fgate"][l]]), (0, LANES - 2 * C_HEADS))
    return dict(
        norm1_g=a["norm1_g"][l], w_all=w_all,
        qg=jnp.tile(a["q_norm_g"][l], A_HEADS).reshape(1, A_Q_COLS),
        qg64=a["q_norm_g"][l].reshape(1, A_HEAD_DIM),
        kg=jnp.tile(a["k_norm_g"][l], A_KV_HEADS).reshape(1, A_KV_COLS),
        sink=a["attn_sink"][l].astype(f32),
        w_pool=a["w_pool"][l].astype(bf16), pool_scale=a["pool_scale"][l].reshape(1, POOL_W),
        gate_bias=gate_bias.reshape(1, LANES).astype(f32),
        hg=jnp.tile(a["h_norm_g"][l], C_HEADS).reshape(1, C_W),
        wa=a["w_br_a"][l].astype(bf16), wb=a["w_br_b"][l].astype(bf16), wc=a["w_br_c"][l].astype(bf16),
        wo=a["w_out"][l].astype(bf16),
        norm2_g=a["norm2_g"][l], wq=a["peer_wq"][l].astype(bf16),
        k1=a["peer_k1"][l].astype(bf16), k2=a["peer_k2"][l].astype(bf16),
        u_tab=_pack_table(a["peer_u"][l]), v_tab=_pack_table(a["peer_v"][l]),
    )


def _constants():
    head = np.arange(A_Q_COLS) // A_HEAD_DIM
    bd = jnp.asarray(head[:, None] == head[None, :], bf16)
    tri = jnp.asarray(np.tril(np.ones((C_CHUNK, C_CHUNK))), bf16)
    sel = jnp.asarray(np.arange(HALF_CHUNKS * PEER_PICKS)[None, :] // HALF_CHUNKS
                      == np.arange(PEER_PICKS)[:, None], bf16)
    return bd, tri, sel


def _layer_prompt(x, p, c):
    n, t, _ = x.shape
    x2d = x.reshape(n * t, D_MODEL)
    z_a, z_pool, z_c, z_if, z_g = _proj(x2d, p["norm1_g"], p["w_all"])
    ya, kn = _swa_prompt(z_a, c["bias_p"], p["sink"], p["qg"], p["kg"], c["bd"], n, t)
    yb = _pool_prompt(z_pool, p["w_pool"], p["pool_scale"], n, t)
    hc, c_new, n_new, m_pad = _mlstm_prompt(z_c, z_if, p["gate_bias"], c["tri"], n, t)
    x2d = _merge(x2d, ya.reshape(n * t, A_Q_COLS), yb.reshape(n * t, POOL_W), hc.reshape(n * t, C_W),
                 z_c, z_g, p["hg"], p["wa"], p["wb"], p["wc"], p["wo"])
    x2d = _peer(x2d, p)
    kv_shape = (n, WINDOW, A_KV_HEADS, A_HEAD_DIM)
    k_buf = kn[:, t - WINDOW:, :].reshape(kv_shape)
    v_buf = z_a.reshape(n, t, -1)[:, t - WINDOW:, A_Q_COLS + A_KV_COLS:].reshape(kv_shape)
    pool_buf = z_pool.reshape(n, t, POOL_W)[:, t - POOL_BUF:, :]
    return x2d.reshape(n, t, D_MODEL), (k_buf, v_buf, pool_buf, c_new, n_new, m_pad[:, 0:C_HEADS, 0])


def _layer_sample(x, st, p, c):
    n = x.shape[0]
    x2d = x.reshape(n, D_MODEL)
    z_a, z_pool, z_c, z_if, z_g = _proj(x2d, p["norm1_g"], p["w_all"])
    ya, k_buf, v_buf = _swa_sample(z_a, st["k"], st["v"], c["bias_s"], p["sink"], p["qg64"], p["kg"],
                                   c["bd"][0:A_KV_COLS, 0:A_KV_COLS])
    yb = _pool_sample(z_pool, st["pool"], p["w_pool"], p["pool_scale"])
    hc, c_new, n_new, m_new = _mlstm_sample(z_c[:, 0:3 * C_W], z_if, p["gate_bias"], st["c"], st["n"], st["m"])
    x2d = _merge(x2d, ya, yb, hc, z_c, z_g, p["hg"], p["wa"], p["wb"], p["wc"], p["wo"])
    x2d = _peer(x2d, p)
    pool_buf = jnp.concatenate([st["pool"][:, 1:, :], z_pool[:, None, :]], axis=1)
    return x2d.reshape(n, 1, D_MODEL), (k_buf, v_buf, pool_buf, c_new, n_new, m_new)


def kernel(x_prompt, x_sample, cache_k, cache_v, state_pool, state_mlstm_c, state_mlstm_n, state_mlstm_m, rel_bias, norm1_g, w_in, q_norm_g, k_norm_g, attn_sink, w_pool, pool_scale, b_igate, b_fgate, h_norm_g, w_br_a, w_br_b, w_br_c, w_out, norm2_g, peer_wq, peer_k1, peer_k2, peer_u, peer_v):
    a = dict(norm1_g=norm1_g, w_in=w_in, q_norm_g=q_norm_g, k_norm_g=k_norm_g, attn_sink=attn_sink, w_pool=w_pool,
             pool_scale=pool_scale, b_igate=b_igate, b_fgate=b_fgate, h_norm_g=h_norm_g, w_br_a=w_br_a,
             w_br_b=w_br_b, w_br_c=w_br_c, w_out=w_out, norm2_g=norm2_g, peer_wq=peer_wq, peer_k1=peer_k1,
             peer_k2=peer_k2, peer_u=peer_u, peer_v=peer_v)
    depth = w_in.shape[0]
    bd, tri, sel = _constants()
    bias_p, bias_s = _bias_tables(rel_bias)
    consts = dict(bd=bd, tri=tri, bias_p=bias_p, bias_s=bias_s)
    xp, xs = x_prompt, x_sample
    new_p = [[] for _ in range(6)]
    new_s = [[] for _ in range(6)]
    for l in range(depth):
        p = _layer_params(l, a)
        p["sel"] = sel
        st_s = dict(k=cache_k[l], v=cache_v[l], pool=state_pool[l], c=state_mlstm_c[l], n=state_mlstm_n[l],
                    m=state_mlstm_m[l])
        xp, sp = _layer_prompt(xp, p, consts)
        xs, ss = _layer_sample(xs, st_s, p, consts)
        for i in range(6):
            new_p[i].append(sp[i])
            new_s[i].append(ss[i])
    outs_p = [jnp.stack(v) for v in new_p]
    outs_s = [jnp.stack(v) for v in new_s]
    return (xp, xs, *outs_p, *outs_s)
```

```python
import functools
import math

import jax
import jax.numpy as jnp
import numpy as np
from jax import lax
from jax.experimental import pallas as pl
from jax.experimental.pallas import tpu as pltpu

f32 = jnp.float32
bf16 = jnp.bfloat16
i32 = jnp.int32

D_MODEL = 1024
A_HEADS = 8
A_KV_HEADS = 2
A_HEAD_DIM = 64
A_GROUP = A_HEADS // A_KV_HEADS
WINDOW = 128
N_BUCKETS = 32
MAX_DISTANCE = 128
A_Q_COLS = A_HEADS * A_HEAD_DIM
A_KV_COLS = A_KV_HEADS * A_HEAD_DIM
POOL_WINDOWS = (2, 4, 8, 16)
POOL_GROUP_W = 128
POOL_W = 512
POOL_BUF = 15
C_HEADS = 4
C_HEAD_DIM = 128
C_W = C_HEADS * C_HEAD_DIM
C_CHUNK = 128
PAST_LEN = 16384
PEER_HEADS = 8
PEER_DK = 256
N_KEYS = 128
N_EXPERTS = N_KEYS * N_KEYS
PEER_TOPK = 16
PEER_PICKS = PEER_HEADS * PEER_TOPK
EPS = 1e-6
NEG_INF = float("-inf")

LANES = 128
ROW_CHUNKS = D_MODEL // LANES
HALF_CHUNKS = ROW_CHUNKS // 2
MIB = 1024 * 1024
PEER_TB = 64
TOK_UNROLL = 8
GROUP_IDX = TOK_UNROLL * PEER_PICKS

PROJ_SPLITS = (A_Q_COLS + 2 * A_KV_COLS, POOL_W, 4 * C_W, LANES, 3 * D_MODEL)

STAIR_ROWS = tuple(PEER_TOPK // (a + 1) for a in range(PEER_TOPK))
STAIR_N = sum(STAIR_ROWS)
STAIR_PAD = (-STAIR_N) % 8


def _cparams(sem, vmem_mib):
    return pltpu.CompilerParams(dimension_semantics=sem, vmem_limit_bytes=vmem_mib * MIB)


def _resident(shape):
    nd = len(shape)
    return pl.BlockSpec(shape, lambda *_: (0,) * nd, pipeline_mode=pl.Buffered(1))


def _dot(a, b):
    return jnp.dot(a, b, preferred_element_type=f32)


def _dot_nt(a, b):
    return lax.dot_general(a, b, (((1,), (1,)), ((), ())), preferred_element_type=f32)


def _dot_tn(a, b):
    return lax.dot_general(a, b, (((0,), (0,)), ((), ())), preferred_element_type=f32)


def _split_bf16(x):
    hi = x.astype(bf16)
    lo = (x - hi.astype(f32)).astype(bf16)
    return hi, lo


def _sigmoid(x):
    return 1.0 / (1.0 + jnp.exp(-x))


def _log_sigmoid(x):
    return jnp.minimum(x, 0.0) - jnp.log(1.0 + jnp.exp(-jnp.abs(x)))


def _rms_rows(x, g):
    ms = jnp.mean(x * x, axis=-1, keepdims=True)
    return x * lax.rsqrt(ms + EPS) * g


def _headnorm(x, blockdiag, g, width):
    hi, lo = _split_bf16(x * x)
    ss = _dot(hi, blockdiag) + _dot(lo, blockdiag)
    return x * lax.rsqrt(ss * (1.0 / width) + EPS) * g


def _proj_kernel(x_ref, g_ref, w_ref, *o_refs):
    xn = _rms_rows(x_ref[...], g_ref[...]).astype(bf16)
    off = 0
    for o_ref, width in zip(o_refs, PROJ_SPLITS):
        o_ref[...] = _dot(xn, w_ref[:, off:off + width])
        off += width


def _proj(x2d, g, w_all):
    m = x2d.shape[0]
    tm = min(256, m)
    total = sum(PROJ_SPLITS)
    return pl.pallas_call(
        _proj_kernel,
        grid=(m // tm,),
        in_specs=[pl.BlockSpec((tm, D_MODEL), lambda i: (i, 0)),
                  _resident((1, D_MODEL)),
                  _resident((D_MODEL, total))],
        out_specs=[pl.BlockSpec((tm, w), lambda i: (i, 0)) for w in PROJ_SPLITS],
        out_shape=[jax.ShapeDtypeStruct((m, w), f32) for w in PROJ_SPLITS],
        compiler_params=_cparams(("parallel",), 48),
        name="proj",
    )(x2d, g.reshape(1, D_MODEL), w_all)


def _swa_kernel(sink_ref, zc_ref, kp_ref, vp_ref, bias_ref, qg_ref, kg_ref, bd_ref, ya_ref, kn_ref):
    i = pl.program_id(1)
    zc = zc_ref[0]
    bd = bd_ref[...]
    bdk = bd[0:A_KV_COLS, 0:A_KV_COLS]
    qn = _headnorm(zc[:, 0:A_Q_COLS], bd, qg_ref[...], A_HEAD_DIM)
    kcn = _headnorm(zc[:, A_Q_COLS:A_Q_COLS + A_KV_COLS], bdk, kg_ref[...], A_HEAD_DIM)
    kpn = _headnorm(kp_ref[0], bdk, kg_ref[...], A_HEAD_DIM)
    kn_ref[0] = kcn
    vc = zc[:, A_Q_COLS + A_KV_COLS:A_Q_COLS + 2 * A_KV_COLS]
    kcat = jnp.concatenate([kpn, kcn], axis=0).astype(bf16)
    vcat = jnp.concatenate([vp_ref[0], vc], axis=0).astype(bf16)
    qb = qn.astype(bf16)
    col = lax.broadcasted_iota(i32, (WINDOW, 2 * WINDOW), 1)
    has_prev = jnp.logical_or(col >= WINDOW, i > 0)
    outs = []
    for h in range(A_HEADS):
        kv = h // A_GROUP
        ks = slice(kv * A_HEAD_DIM, (kv + 1) * A_HEAD_DIM)
        logits = _dot_nt(qb[:, h * A_HEAD_DIM:(h + 1) * A_HEAD_DIM], kcat[:, ks]) * (A_HEAD_DIM ** -0.5)
        logits = jnp.where(has_prev, logits + bias_ref[h], NEG_INF)
        sink = sink_ref[h]
        mx = jnp.maximum(jnp.max(logits, axis=-1, keepdims=True), sink)
        p = jnp.exp(logits - mx)
        p = p / (jnp.sum(p, axis=-1, keepdims=True) + jnp.exp(sink - mx))
        outs.append(_dot(p.astype(bf16), vcat[:, ks]))
    ya_ref[0] = jnp.concatenate(outs, axis=-1)


def _swa_prompt(z_a, bias_tab, sink, qg, kg, bd, n, t):
    nb = t // WINDOW
    z3 = z_a.reshape(n, t, PROJ_SPLITS[0])
    kblk = A_Q_COLS // A_KV_COLS
    prev = lambda b, i: (b, jnp.maximum(i - 1, 0), kblk)
    prev_v = lambda b, i: (b, jnp.maximum(i - 1, 0), kblk + 1)
    ya, kn = pl.pallas_call(
        _swa_kernel,
        grid=(n, nb),
        in_specs=[pl.BlockSpec(memory_space=pltpu.SMEM),
                  pl.BlockSpec((1, WINDOW, PROJ_SPLITS[0]), lambda b, i: (b, i, 0)),
                  pl.BlockSpec((1, WINDOW, A_KV_COLS), prev),
                  pl.BlockSpec((1, WINDOW, A_KV_COLS), prev_v),
                  _resident((A_HEADS, WINDOW, 2 * WINDOW)),
                  _resident((1, A_Q_COLS)),
                  _resident((1, A_KV_COLS)),
                  _resident((A_Q_COLS, A_Q_COLS))],
        out_specs=[pl.BlockSpec((1, WINDOW, A_Q_COLS), lambda b, i: (b, i, 0)),
                   pl.BlockSpec((1, WINDOW, A_KV_COLS), lambda b, i: (b, i, 0))],
        out_shape=[jax.ShapeDtypeStruct((n, t, A_Q_COLS), f32),
                   jax.ShapeDtypeStruct((n, t, A_KV_COLS), f32)],
        compiler_params=_cparams(("parallel", "arbitrary"), 32),
        name="swa_prompt",
    )(sink, z3, z3, z3, bias_tab, qg, kg, bd)
    return ya, kn


def _swa_sample_kernel(q_ref, kn_ref, vn_ref, ck_ref, cv_ref, bias_ref, sink_ref, qg_ref, kg_ref, bd_ref,
                       ya_ref, ko_ref, vo_ref):
    q = q_ref[...]
    qn = q * lax.rsqrt(jnp.mean(q * q, axis=-1, keepdims=True) + EPS) * qg_ref[...]
    kn = _headnorm(kn_ref[...], bd_ref[...], kg_ref[...], A_HEAD_DIM)
    rowi = lax.broadcasted_iota(i32, ck_ref.shape, 1)
    newest = rowi == WINDOW - 1
    k_all = jnp.where(newest, kn[:, None, :], pltpu.roll(ck_ref[...], WINDOW - 1, 1))
    v_all = jnp.where(newest, vn_ref[...][:, None, :], pltpu.roll(cv_ref[...], WINDOW - 1, 1))
    ko_ref[...] = k_all
    vo_ref[...] = v_all
    outs = []
    for kv in range(A_KV_HEADS):
        ks = slice(kv * A_HEAD_DIM, (kv + 1) * A_HEAD_DIM)
        hs = slice(kv * A_GROUP, (kv + 1) * A_GROUP)
        kh = k_all[:, :, ks].astype(bf16)
        vh = v_all[:, :, ks].astype(bf16)
        qh = qn[:, hs, :].astype(bf16)
        logits = jnp.einsum('bgd,bsd->bgs', qh, kh, preferred_element_type=f32) * (A_HEAD_DIM ** -0.5)
        logits = logits + bias_ref[hs, :][None]
        sink = sink_ref[hs, :][None]
        mx = jnp.maximum(jnp.max(logits, axis=-1, keepdims=True), sink)
        p = jnp.exp(logits - mx)
        p = p / (jnp.sum(p, axis=-1, keepdims=True) + jnp.exp(sink - mx))
        outs.append(jnp.einsum('bgs,bsd->bgd', p.astype(bf16), vh, preferred_element_type=f32))
    ya_ref[...] = jnp.concatenate(outs, axis=1)


def _swa_sample(z_a, cache_k, cache_v, bias_s, sink, qg64, kg, bdk):
    n = z_a.shape[0]
    bn = 8
    q3 = z_a[:, 0:A_Q_COLS].reshape(n, A_HEADS, A_HEAD_DIM)
    k_new = z_a[:, A_Q_COLS:A_Q_COLS + A_KV_COLS]
    v_new = z_a[:, A_Q_COLS + A_KV_COLS:]
    ck = cache_k.reshape(n, WINDOW, A_KV_COLS)
    cv = cache_v.reshape(n, WINDOW, A_KV_COLS)
    row2 = pl.BlockSpec((bn, A_KV_COLS), lambda i: (i, 0))
    cache = pl.BlockSpec((bn, WINDOW, A_KV_COLS), lambda i: (i, 0, 0))
    q_spec = pl.BlockSpec((bn, A_HEADS, A_HEAD_DIM), lambda i: (i, 0, 0))
    ya, ko, vo = pl.pallas_call(
        _swa_sample_kernel,
        grid=(n // bn,),
        in_specs=[q_spec, row2, row2, cache, cache,
                  _resident((A_HEADS, WINDOW)), _resident((A_HEADS, 1)),
                  _resident((1, A_HEAD_DIM)), _resident((1, A_KV_COLS)), _resident((A_KV_COLS, A_KV_COLS))],
        out_specs=[q_spec, cache, cache],
        out_shape=[jax.ShapeDtypeStruct((n, A_HEADS, A_HEAD_DIM), f32),
                   jax.ShapeDtypeStruct((n, WINDOW, A_KV_COLS), f32),
                   jax.ShapeDtypeStruct((n, WINDOW, A_KV_COLS), f32)],
        compiler_params=_cparams(("parallel",), 32),
        name="swa_sample",
    )(q3, k_new, v_new, ck, cv, bias_s, sink.reshape(A_HEADS, 1), qg64, kg, bdk)
    shape = (n, WINDOW, A_KV_HEADS, A_HEAD_DIM)
    return ya.reshape(n, A_Q_COLS), ko.reshape(shape), vo.reshape(shape)


def _pool_kernel(x_ref, w_ref, scale_ref, y_ref, buf_ref):
    j = pl.program_id(1)
    bt = x_ref.shape[1]
    halo = POOL_BUF + 1

    @pl.when(j == 0)
    def _():
        buf_ref[0:halo, :] = jnp.zeros((halo, POOL_W), f32)

    x = x_ref[0]
    buf_ref[halo:halo + bt, :] = x
    pos = j * bt + lax.broadcasted_iota(i32, (bt, 1), 0)
    outs = []
    for g, w in enumerate(POOL_WINDOWS):
        cs = slice(g * POOL_GROUP_W, (g + 1) * POOL_GROUP_W)
        xg = x[:, cs]
        s = xg
        for k in range(1, w):
            s = s + buf_ref[halo - k:halo - k + bt, cs]
        cnt = jnp.minimum(pos + 1, w).astype(f32)
        pooled = s / cnt - xg
        outs.append(_dot(pooled.astype(bf16), w_ref[g]))
    y_ref[0] = jnp.concatenate(outs, axis=-1) * scale_ref[...]
    buf_ref[0:halo, :] = x[bt - halo:bt, :]


def _pool_prompt(z_pool, w_pool, scale, n, t):
    bt = min(512, t)
    z3 = z_pool.reshape(n, t, POOL_W)
    spec = pl.BlockSpec((1, bt, POOL_W), lambda b, j: (b, j, 0))
    return pl.pallas_call(
        _pool_kernel,
        grid=(n, t // bt),
        in_specs=[spec, _resident((4, POOL_GROUP_W, POOL_GROUP_W)), _resident((1, POOL_W))],
        out_specs=spec,
        out_shape=jax.ShapeDtypeStruct((n, t, POOL_W), f32),
        scratch_shapes=[pltpu.VMEM((bt + POOL_BUF + 1, POOL_W), f32)],
        compiler_params=_cparams(("parallel", "arbitrary"), 32),
        name="pool_prompt",
    )(z3, w_pool, scale)


def _pool_sample_kernel(x_ref, st_ref, w_ref, scale_ref, y_ref):
    x = x_ref[...]
    outs = []
    for g, w in enumerate(POOL_WINDOWS):
        cs = slice(g * POOL_GROUP_W, (g + 1) * POOL_GROUP_W)
        xg = x[:, cs]
        s = xg
        for k in range(1, w):
            s = s + st_ref[:, POOL_BUF - k, cs]
        cnt = float(min(PAST_LEN + 1, w))
        pooled = s / cnt - xg
        outs.append(_dot(pooled.astype(bf16), w_ref[g]))
    y_ref[...] = jnp.concatenate(outs, axis=-1) * scale_ref[...]


def _pool_sample(z_pool, state, w_pool, scale):
    n = z_pool.shape[0]
    bn = 8
    return pl.pallas_call(
        _pool_sample_kernel,
        grid=(n // bn,),
        in_specs=[pl.BlockSpec((bn, POOL_W), lambda i: (i, 0)),
                  pl.BlockSpec((bn, POOL_BUF, POOL_W), lambda i: (i, 0, 0)),
                  _resident((4, POOL_GROUP_W, POOL_GROUP_W)), _resident((1, POOL_W))],
        out_specs=pl.BlockSpec((bn, POOL_W), lambda i: (i, 0)),
        out_shape=jax.ShapeDtypeStruct((n, POOL_W), f32),
        compiler_params=_cparams(("parallel",), 32),
        name="pool_sample",
    )(z_pool, state, w_pool, scale)


def _mlstm_kernel(zc_ref, zif_ref, gb_ref, tri_ref, h_ref, c_out, n_out, m_out, c_s, n_s, m_s):
    j = pl.program_id(1)
    L = C_CHUNK

    @pl.when(j == 0)
    def _():
        c_s[...] = jnp.zeros(c_s.shape, f32)
        n_s[...] = jnp.zeros(n_s.shape, f32)
        m_s[...] = jnp.zeros(m_s.shape, f32)

    zc = zc_ref[0]
    gates = zif_ref[0] + gb_ref[...]
    lf_hi, lf_lo = _split_bf16(_log_sigmoid(gates))
    bcum = _dot(tri_ref[...], lf_hi) + _dot(tri_ref[...], lf_lo)
    bcum_t = bcum.T
    gates_t = gates.T
    row = lax.broadcasted_iota(i32, (L, L), 0)
    col = lax.broadcasted_iota(i32, (L, L), 1)
    causal = col <= row
    for h in range(C_HEADS):
        hs = slice(h * C_HEAD_DIM, (h + 1) * C_HEAD_DIM)
        q = zc[:, hs]
        k = zc[:, C_W + h * C_HEAD_DIM:C_W + (h + 1) * C_HEAD_DIM] * (C_HEAD_DIM ** -0.5)
        v = zc[:, 2 * C_W + h * C_HEAD_DIM:2 * C_W + (h + 1) * C_HEAD_DIM]
        b_col = bcum[:, C_HEADS + h:C_HEADS + h + 1]
        b_row = bcum_t[C_HEADS + h:C_HEADS + h + 1, :]
        ig_col = gates[:, h:h + 1]
        ig_row = gates_t[h:h + 1, :]
        m_prev = m_s[h:h + 1, 0:1]
        dmat = jnp.where(causal, b_col - b_row + ig_row, NEG_INF)
        inter = b_col + m_prev
        m_t = jnp.maximum(inter, jnp.max(dmat, axis=-1, keepdims=True))
        qb, kb, vb = q.astype(bf16), k.astype(bf16), v.astype(bf16)
        s = _dot_nt(qb, kb) * jnp.exp(dmat - m_t)
        w_inter = jnp.exp(inter - m_t)
        c_prev = c_s[h]
        n_prev = n_s[h:h + 1, :]
        num = _dot(s.astype(bf16), vb) + w_inter * _dot_nt(qb, c_prev.astype(bf16))
        den = jnp.sum(s, axis=-1, keepdims=True) + w_inter * jnp.sum(q * n_prev, axis=-1, keepdims=True)
        h_ref[0, :, hs] = num / jnp.maximum(jnp.abs(den), jnp.exp(-m_t))
        m_new = m_t[L - 1:L, :]
        b_last = b_col[L - 1:L, :]
        w_end = jnp.exp(b_last - b_col + ig_col - m_new)
        decay = jnp.exp(b_last + m_prev - m_new)
        c_s[h] = decay * c_prev + _dot_tn((v * w_end).astype(bf16), kb)
        n_s[h:h + 1, :] = decay * n_prev + jnp.sum(k * w_end, axis=0, keepdims=True)
        m_s[h:h + 1, :] = jnp.broadcast_to(m_new, (1, LANES))

    @pl.when(j == pl.num_programs(1) - 1)
    def _():
        c_out[0] = c_s[...]
        n_out[0] = n_s[0:C_HEADS, :]
        m_out[0] = m_s[...]


def _mlstm_prompt(z_c, z_if, gate_bias, tri, n, t):
    nc = t // C_CHUNK
    zc3 = z_c.reshape(n, t, 4 * C_W)
    zif3 = z_if.reshape(n, t, LANES)
    return pl.pallas_call(
        _mlstm_kernel,
        grid=(n, nc),
        in_specs=[pl.BlockSpec((1, C_CHUNK, 3 * C_W), lambda b, j: (b, j, 0)),
                  pl.BlockSpec((1, C_CHUNK, LANES), lambda b, j: (b, j, 0)),
                  _resident((1, LANES)),
                  _resident((C_CHUNK, C_CHUNK))],
        out_specs=[pl.BlockSpec((1, C_CHUNK, C_W), lambda b, j: (b, j, 0)),
                   pl.BlockSpec((1, C_HEADS, C_HEAD_DIM, C_HEAD_DIM), lambda b, j: (b, 0, 0, 0)),
                   pl.BlockSpec((1, C_HEADS, C_HEAD_DIM), lambda b, j: (b, 0, 0)),
                   pl.BlockSpec((1, 8, LANES), lambda b, j: (b, 0, 0))],
        out_shape=[jax.ShapeDtypeStruct((n, t, C_W), f32),
                   jax.ShapeDtypeStruct((n, C_HEADS, C_HEAD_DIM, C_HEAD_DIM), f32),
                   jax.ShapeDtypeStruct((n, C_HEADS, C_HEAD_DIM), f32),
                   jax.ShapeDtypeStruct((n, 8, LANES), f32)],
        scratch_shapes=[pltpu.VMEM((C_HEADS, C_HEAD_DIM, C_HEAD_DIM), f32),
                        pltpu.VMEM((8, LANES), f32),
                        pltpu.VMEM((8, LANES), f32)],
        compiler_params=_cparams(("parallel", "arbitrary"), 32),
        name="mlstm_prompt",
    )(zc3, zif3, gate_bias, tri)


def _mlstm_sample_kernel(zc_ref, zif_ref, gb_ref, c_ref, n_ref, m_ref, h_ref, co_ref, no_ref, mo_ref):
    bn = zc_ref.shape[0]
    zc = zc_ref[...]
    gates = zif_ref[...] + gb_ref[...]
    lfs = _log_sigmoid(gates)
    lane = lax.broadcasted_iota(i32, (C_HEAD_DIM, LANES), 1)
    pad = jnp.zeros((LANES - bn, C_HEAD_DIM), f32)
    for h in range(C_HEADS):
        hs = slice(h * C_HEAD_DIM, (h + 1) * C_HEAD_DIM)
        q = zc[:, hs]
        k = zc[:, C_W + h * C_HEAD_DIM:C_W + (h + 1) * C_HEAD_DIM] * (C_HEAD_DIM ** -0.5)
        v = zc[:, 2 * C_W + h * C_HEAD_DIM:2 * C_W + (h + 1) * C_HEAD_DIM]
        ig = gates[:, h:h + 1]
        lf = lfs[:, C_HEADS + h:C_HEADS + h + 1]
        m_prev = m_ref[:, h:h + 1]
        n_prev = n_ref[:, h, :]
        inter = lf + m_prev
        m_t = jnp.maximum(inter, ig)
        s = jnp.sum(q * k, axis=-1, keepdims=True) * jnp.exp(ig - m_t)
        w_inter = jnp.exp(inter - m_t)
        w_end = jnp.exp(ig - m_t)
        cq_t = jnp.zeros((C_HEAD_DIM, LANES), f32)
        for b in range(bn):
            colsum = jnp.sum(c_ref[b, h] * q[b:b + 1, :], axis=-1, keepdims=True)
            cq_t = jnp.where(lane == b, colsum, cq_t)
        cq = cq_t.T[0:bn, :]
        num = s * v + w_inter * cq
        den = s + w_inter * jnp.sum(n_prev * q, axis=-1, keepdims=True)
        h_ref[:, hs] = num / jnp.maximum(jnp.abs(den), jnp.exp(-m_t))
        vw_t = jnp.concatenate([v * w_end, pad], axis=0).T
        for b in range(bn):
            co_ref[b, h] = w_inter[b:b + 1, :] * c_ref[b, h] + vw_t[:, b:b + 1] * k[b:b + 1, :]
        no_ref[:, h, :] = w_inter * n_prev + w_end * k
        mo_ref[:, h:h + 1] = m_t


def _mlstm_sample(z_c, z_if, gate_bias, c0, n0, m0):
    n = z_c.shape[0]
    bn = 8
    c_spec = pl.BlockSpec((bn, C_HEADS, C_HEAD_DIM, C_HEAD_DIM), lambda i: (i, 0, 0, 0))
    n_spec = pl.BlockSpec((bn, C_HEADS, C_HEAD_DIM), lambda i: (i, 0, 0))
    m_spec = pl.BlockSpec((bn, C_HEADS), lambda i: (i, 0))
    return pl.pallas_call(
        _mlstm_sample_kernel,
        grid=(n // bn,),
        in_specs=[pl.BlockSpec((bn, 3 * C_W), lambda i: (i, 0)),
                  pl.BlockSpec((bn, LANES), lambda i: (i, 0)),
                  _resident((1, LANES)), c_spec, n_spec, m_spec],
        out_specs=[pl.BlockSpec((bn, C_W), lambda i: (i, 0)), c_spec, n_spec, m_spec],
        out_shape=[jax.ShapeDtypeStruct((n, C_W), f32),
                   jax.ShapeDtypeStruct(c0.shape, f32),
                   jax.ShapeDtypeStruct(n0.shape, f32),
                   jax.ShapeDtypeStruct(m0.shape, f32)],
        compiler_params=_cparams(("parallel",), 32),
        name="mlstm_sample",
    )(z_c, z_if, gate_bias, c0, n0, m0)


def _merge_kernel(x_ref, ya_ref, yb_ref, h_ref, co_ref, zg_ref, hg_ref, wa_ref, wb_ref, wc_ref, wo_ref, o_ref):
    h = h_ref[...]
    parts = []
    for hh in range(C_HEADS):
        hs = h[:, hh * C_HEAD_DIM:(hh + 1) * C_HEAD_DIM]
        parts.append(hs * lax.rsqrt(jnp.mean(hs * hs, axis=-1, keepdims=True) + EPS))
    yc = jnp.concatenate(parts, axis=-1) * hg_ref[...] * _sigmoid(co_ref[...])
    zg = zg_ref[...]
    merged = (_sigmoid(zg[:, 0:D_MODEL]) * _dot(ya_ref[...].astype(bf16), wa_ref[...])
              + _sigmoid(zg[:, D_MODEL:2 * D_MODEL]) * _dot(yb_ref[...].astype(bf16), wb_ref[...])
              + _sigmoid(zg[:, 2 * D_MODEL:]) * _dot(yc.astype(bf16), wc_ref[...]))
    o_ref[...] = x_ref[...] + _dot(merged.astype(bf16), wo_ref[...])


def _merge(x2d, ya, yb, hc, z_c, z_g, hg, wa, wb, wc, wo):
    m = x2d.shape[0]
    tb = min(256, m)
    row = lambda w: pl.BlockSpec((tb, w), lambda i: (i, 0))
    return pl.pallas_call(
        _merge_kernel,
        grid=(m // tb,),
        in_specs=[row(D_MODEL), row(A_Q_COLS), row(POOL_W), row(C_W),
                  pl.BlockSpec((tb, C_W), lambda i: (i, 3)),
                  row(3 * D_MODEL), _resident((1, C_W)),
                  _resident((A_Q_COLS, D_MODEL)), _resident((POOL_W, D_MODEL)), _resident((C_W, D_MODEL)),
                  _resident((D_MODEL, D_MODEL))],
        out_specs=row(D_MODEL),
        out_shape=jax.ShapeDtypeStruct((m, D_MODEL), f32),
        compiler_params=_cparams(("parallel",), 48),
        name="merge",
    )(x2d, ya, yb, hc, z_c, z_g, hg, wa, wb, wc, wo)


def _topk_rows(s, k):
    nrows = s.shape[0]
    riota = lax.broadcasted_iota(i32, s.shape, 0)
    vals, idxs = [], []
    for _ in range(k):
        m = jnp.max(s, axis=0, keepdims=True)
        i = jnp.min(jnp.where(s == m, riota, nrows), axis=0, keepdims=True)
        vals.append(m)
        idxs.append(i)
        s = jnp.where(riota == i, NEG_INF, s)
    return vals, idxs


def _route_kernel(x_ref, g_ref, wq_ref, k1_ref, k2_ref, idx_ref, gate_ref, xn_ref):
    tb = x_ref.shape[0]
    xn = _rms_rows(x_ref[...], g_ref[...])
    xn_ref[...] = xn
    qb = _dot(xn.astype(bf16), wq_ref[...]).astype(bf16)
    half = PEER_DK // 2
    exp_rows, gate_rows = [], []
    for h in range(PEER_HEADS):
        s1 = _dot_nt(k1_ref[...], qb[:, h * PEER_DK:h * PEER_DK + half])
        s2 = _dot_nt(k2_ref[...], qb[:, h * PEER_DK + half:(h + 1) * PEER_DK])
        v1, i1 = _topk_rows(s1, PEER_TOPK)
        v2, i2 = _topk_rows(s2, PEER_TOPK)
        v2s = jnp.concatenate(v2, axis=0)
        i2s = jnp.concatenate(i2, axis=0)
        cv, ce = [], []
        for a, nb in enumerate(STAIR_ROWS):
            cv.append(v1[a] + v2s[0:nb])
            ce.append((i1[a] * N_KEYS + i2s[0:nb]) * HALF_CHUNKS)
        if STAIR_PAD:
            cv.append(jnp.full((STAIR_PAD, tb), NEG_INF, f32))
            ce.append(jnp.zeros((STAIR_PAD, tb), i32))
        cand = jnp.concatenate(cv, axis=0)
        cexp = jnp.concatenate(ce, axis=0)
        riota = lax.broadcasted_iota(i32, cand.shape, 0)
        tv, ti = _topk_rows(cand, PEER_TOPK)
        te = [jnp.max(jnp.where(riota == i, cexp, -1), axis=0, keepdims=True) for i in ti]
        tvs = jnp.concatenate(tv, axis=0)
        ex = jnp.exp(tvs - tv[0])
        gate_rows.append(ex / jnp.sum(ex, axis=0, keepdims=True))
        exp_rows.append(jnp.concatenate(te, axis=0))
    idx_ref[...] = jnp.concatenate(exp_rows, axis=0).T
    gate_ref[...] = jnp.concatenate(gate_rows, axis=0).T


def _route(x2d, g, wq, k1, k2):
    m = x2d.shape[0]
    tb = min(256, m)
    return pl.pallas_call(
        _route_kernel,
        grid=(m // tb,),
        in_specs=[pl.BlockSpec((tb, D_MODEL), lambda i: (i, 0)),
                  _resident((1, D_MODEL)),
                  _resident((D_MODEL, PEER_HEADS * PEER_DK)),
                  _resident((N_KEYS, PEER_DK // 2)), _resident((N_KEYS, PEER_DK // 2))],
        out_specs=[pl.BlockSpec((tb, PEER_PICKS), lambda i: (i, 0)),
                   pl.BlockSpec((tb, PEER_PICKS), lambda i: (i, 0)),
                   pl.BlockSpec((tb, D_MODEL), lambda i: (i, 0))],
        out_shape=[jax.ShapeDtypeStruct((m, PEER_PICKS), i32),
                   jax.ShapeDtypeStruct((m, PEER_PICKS), f32),
                   jax.ShapeDtypeStruct((m, D_MODEL), f32)],
        compiler_params=_cparams(("parallel",), 48),
        name="peer_route",
    )(x2d, g.reshape(1, D_MODEL), wq, k1, k2)


def _pack_rows(w):
    bits = lax.bitcast_convert_type(w.astype(bf16), jnp.uint16).astype(jnp.uint32)
    bits = bits.reshape(w.shape[0], HALF_CHUNKS, 2, LANES)
    packed = bits[:, :, 0, :] | (bits[:, :, 1, :] << 16)
    return lax.bitcast_convert_type(packed, i32)


def _gather_groups(idx_hbm, idx_bufs, sem, tb, group_fn):
    assert tb % (2 * TOK_UNROLL) == 0, tb
    ngroups = tb // TOK_UNROLL
    npairs = ngroups // 2
    first = pl.program_id(0) * ngroups

    def copy(grp, slot):
        src = idx_hbm.at[pl.ds((first + grp) * GROUP_IDX, GROUP_IDX)]
        return pltpu.make_async_copy(src, idx_bufs[slot], sem.at[slot])

    copy(0, 0).start()

    def pair(i2, carry):
        copy(2 * i2 + 1, 1).start()
        copy(2 * i2, 0).wait()
        group_fn(pl.multiple_of(2 * i2 * TOK_UNROLL, TOK_UNROLL), idx_bufs[0])

        @pl.when(i2 + 1 < npairs)
        def _():
            copy(2 * i2 + 2, 0).start()

        copy(2 * i2 + 1, 1).wait()
        group_fn(pl.multiple_of((2 * i2 + 1) * TOK_UNROLL, TOK_UNROLL), idx_bufs[1])
        return carry

    lax.fori_loop(0, npairs, pair, 0)


def _gather_rows(tab_ref, idx_smem, c, g_ref):
    for j in range(PEER_PICKS):
        off = pl.multiple_of(idx_smem[c * PEER_PICKS + j], HALF_CHUNKS)
        g_ref[HALF_CHUNKS * j:HALF_CHUNKS * (j + 1), :] = tab_ref[pl.ds(off, HALF_CHUNKS), :]


def _chunk_diag():
    lane = lax.broadcasted_iota(i32, (ROW_CHUNKS, ROW_CHUNKS * PEER_PICKS), 1)
    sub = lax.broadcasted_iota(i32, (ROW_CHUNKS, ROW_CHUNKS * PEER_PICKS), 0)
    return (lane & (ROW_CHUNKS - 1)) == sub, sub


def _peer_u_kernel(idx_hbm, x_ref, gate_ref, selt_ref, tab_ref, w_ref, g0_ref, g1_ref, s0_ref, s1_ref, sem):
    diag, sub = _chunk_diag()
    gbufs = (g0_ref, g1_ref)

    def group(t0, idx_smem):
        acc = jnp.zeros((TOK_UNROLL, ROW_CHUNKS * PEER_PICKS), f32)
        for c in range(TOK_UNROLL):
            g_ref = gbufs[c % 2]
            _gather_rows(tab_ref, idx_smem, c, g_ref)
            xb = pltpu.bitcast(x_ref[t0 + c], bf16)
            out = _dot_nt(xb, pltpu.bitcast(g_ref[...], bf16))
            rowsum = jnp.sum(jnp.where(diag, out, 0.0), axis=0, keepdims=True)
            acc = jnp.where(sub == c, rowsum, acc)
        hi, lo = _split_bf16(acc)
        a = _dot(hi, selt_ref[...]) + _dot(lo, selt_ref[...])
        gelu = 0.5 * a * (1.0 + lax.erf(a * (2.0 ** -0.5)))
        w_ref[pl.ds(t0, TOK_UNROLL), :] = gate_ref[pl.ds(t0, TOK_UNROLL), :] * gelu

    _gather_groups(idx_hbm, (s0_ref, s1_ref), sem, x_ref.shape[0], group)


def _peer_scratch():
    return [pltpu.VMEM((HALF_CHUNKS * PEER_PICKS, LANES), i32),
            pltpu.VMEM((HALF_CHUNKS * PEER_PICKS, LANES), i32),
            pltpu.SMEM((GROUP_IDX,), i32),
            pltpu.SMEM((GROUP_IDX,), i32),
            pltpu.SemaphoreType.DMA((2,))]


def _peer_u(idx_flat, xp, gates, selt, tab):
    m = xp.shape[0]
    tb = min(PEER_TB, m)
    return pl.pallas_call(
        _peer_u_kernel,
        grid=(m // tb,),
        in_specs=[pl.BlockSpec(memory_space=pl.ANY),
                  pl.BlockSpec((tb, HALF_CHUNKS, LANES), lambda i: (i, 0, 0)),
                  pl.BlockSpec((tb, PEER_PICKS), lambda i: (i, 0)),
                  _resident((ROW_CHUNKS * PEER_PICKS, PEER_PICKS)),
                  _resident((N_EXPERTS * HALF_CHUNKS, LANES))],
        out_specs=pl.BlockSpec((tb, PEER_PICKS), lambda i: (i, 0)),
        out_shape=jax.ShapeDtypeStruct((m, PEER_PICKS), f32),
        scratch_shapes=_peer_scratch(),
        compiler_params=_cparams(("arbitrary",), 44),
        name="peer_u",
    )(idx_flat, xp, gates, selt, tab)


def _peer_v_kernel(idx_hbm, w_ref, x_ref, sel_ref, tab_ref, o_ref, g0_ref, g1_ref, s0_ref, s1_ref, sem):
    diag, _ = _chunk_diag()
    gbufs = (g0_ref, g1_ref)

    def group(t0, idx_smem):
        w8 = _dot(w_ref[pl.ds(t0, TOK_UNROLL), :].astype(bf16), sel_ref[...])
        for c in range(TOK_UNROLL):
            g_ref = gbufs[c % 2]
            _gather_rows(tab_ref, idx_smem, c, g_ref)
            wsel = jnp.where(diag, w8[c:c + 1, :], 0.0).astype(bf16)
            y = _dot(wsel, pltpu.bitcast(g_ref[...], bf16))
            o_ref[t0 + c] = x_ref[t0 + c] + y

    _gather_groups(idx_hbm, (s0_ref, s1_ref), sem, x_ref.shape[0], group)


def _peer_v(idx_flat, w2d, x3, sel, tab):
    m = x3.shape[0]
    tb = min(PEER_TB, m)
    return pl.pallas_call(
        _peer_v_kernel,
        grid=(m // tb,),
        in_specs=[pl.BlockSpec(memory_space=pl.ANY),
                  pl.BlockSpec((tb, PEER_PICKS), lambda i: (i, 0)),
                  pl.BlockSpec((tb, ROW_CHUNKS, LANES), lambda i: (i, 0, 0)),
                  _resident((PEER_PICKS, ROW_CHUNKS * PEER_PICKS)),
                  _resident((N_EXPERTS * HALF_CHUNKS, LANES))],
        out_specs=pl.BlockSpec((tb, ROW_CHUNKS, LANES), lambda i: (i, 0, 0)),
        out_shape=jax.ShapeDtypeStruct((m, ROW_CHUNKS, LANES), f32),
        scratch_shapes=_peer_scratch(),
        compiler_params=_cparams(("arbitrary",), 44),
        name="peer_v",
    )(idx_flat, w2d, x3, sel, tab)


def _peer(x2d, p):
    m = x2d.shape[0]
    idx, gates, xn = _route(x2d, p["norm2_g"], p["wq"], p["k1"], p["k2"])
    idx_flat = idx.reshape(m * PEER_PICKS)
    w2d = _peer_u(idx_flat, _pack_rows(xn), gates, p["selt"], p["u_tab"])
    y3 = _peer_v(idx_flat, w2d, x2d.reshape(m, ROW_CHUNKS, LANES), p["sel"], p["v_tab"])
    return y3.reshape(m, D_MODEL)


def _rel_bucket(dist):
    max_exact = N_BUCKETS // 2
    d = jnp.maximum(dist, 0)
    df = jnp.maximum(d, 1).astype(f32)
    large = max_exact + (jnp.log(df / max_exact) / math.log(MAX_DISTANCE / max_exact)
                         * (N_BUCKETS - max_exact)).astype(i32)
    large = jnp.minimum(large, N_BUCKETS - 1)
    return jnp.where(d < max_exact, d, large)


def _bias_tables(rel_bias):
    dist = jnp.arange(WINDOW)[:, None] + WINDOW - jnp.arange(2 * WINDOW)[None, :]
    bias = jnp.moveaxis(rel_bias[_rel_bucket(dist)].astype(f32), -1, 0)
    bias_p = jnp.where(((dist >= 0) & (dist < WINDOW))[None], bias, NEG_INF)
    dist_s = WINDOW - 1 - jnp.arange(WINDOW)
    bias_s = rel_bias[_rel_bucket(dist_s)].astype(f32).T
    return bias_p, bias_s


def _layer_params(l, a):
    w = a["w_in"][l]
    o_pin = A_Q_COLS + 2 * A_KV_COLS
    o_c = o_pin + POOL_W
    o_if = o_c + 3 * C_W
    o_co = o_if + 2 * C_HEADS
    o_g = o_co + C_W
    w_if = jnp.pad(w[:, o_if:o_co], ((0, 0), (0, LANES - 2 * C_HEADS)))
    w_all = jnp.concatenate([w[:, 0:o_pin], w[:, o_pin:o_c], w[:, o_c:o_if], w[:, o_co:o_g], w_if, w[:, o_g:]],
                            axis=1).astype(bf16)
    gate_bias = jnp.pad(jnp.concatenate([a["b_igate"][l], a["b_fgate"][l]]), (0, LANES - 2 * C_HEADS))
    return dict(
        norm1_g=a["norm1_g"][l], w_all=w_all,
        qg=jnp.tile(a["q_norm_g"][l], A_HEADS).reshape(1, A_Q_COLS),
        qg64=a["q_norm_g"][l].reshape(1, A_HEAD_DIM),
        kg=jnp.tile(a["k_norm_g"][l], A_KV_HEADS).reshape(1, A_KV_COLS),
        sink=a["attn_sink"][l].astype(f32),
        w_pool=a["w_pool"][l].astype(bf16), pool_scale=a["pool_scale"][l].reshape(1, POOL_W),
        gate_bias=gate_bias.reshape(1, LANES).astype(f32),
        hg=jnp.tile(a["h_norm_g"][l], C_HEADS).reshape(1, C_W),
        wa=a["w_br_a"][l].astype(bf16), wb=a["w_br_b"][l].astype(bf16), wc=a["w_br_c"][l].astype(bf16),
        wo=a["w_out"][l].astype(bf16),
        norm2_g=a["norm2_g"][l], wq=a["peer_wq"][l].astype(bf16),
        k1=a["peer_k1"][l].astype(bf16), k2=a["peer_k2"][l].astype(bf16),
        u_tab=_pack_rows(a["peer_u"][l]).reshape(N_EXPERTS * HALF_CHUNKS, LANES),
        v_tab=_pack_rows(a["peer_v"][l]).reshape(N_EXPERTS * HALF_CHUNKS, LANES),
    )


def _constants():
    head = np.arange(A_Q_COLS) // A_HEAD_DIM
    bd = jnp.asarray(head[:, None] == head[None, :], bf16)
    tri = jnp.asarray(np.tril(np.ones((C_CHUNK, C_CHUNK))), bf16)
    sel = jnp.asarray(np.arange(ROW_CHUNKS * PEER_PICKS)[None, :] // ROW_CHUNKS
                      == np.arange(PEER_PICKS)[:, None], bf16)
    return bd, tri, sel


def _layer_prompt(x, p, c):
    n, t, _ = x.shape
    x2d = x.reshape(n * t, D_MODEL)
    z_a, z_pool, z_c, z_if, z_g = _proj(x2d, p["norm1_g"], p["w_all"])
    ya, kn = _swa_prompt(z_a, c["bias_p"], p["sink"], p["qg"], p["kg"], c["bd"], n, t)
    yb = _pool_prompt(z_pool, p["w_pool"], p["pool_scale"], n, t)
    hc, c_new, n_new, m_pad = _mlstm_prompt(z_c, z_if, p["gate_bias"], c["tri"], n, t)
    x2d = _merge(x2d, ya.reshape(n * t, A_Q_COLS), yb.reshape(n * t, POOL_W), hc.reshape(n * t, C_W),
                 z_c, z_g, p["hg"], p["wa"], p["wb"], p["wc"], p["wo"])
    x2d = _peer(x2d, p)
    kv_shape = (n, WINDOW, A_KV_HEADS, A_HEAD_DIM)
    k_buf = kn[:, t - WINDOW:, :].reshape(kv_shape)
    v_buf = z_a.reshape(n, t, -1)[:, t - WINDOW:, A_Q_COLS + A_KV_COLS:].reshape(kv_shape)
    pool_buf = z_pool.reshape(n, t, POOL_W)[:, t - POOL_BUF:, :]
    return x2d.reshape(n, t, D_MODEL), (k_buf, v_buf, pool_buf, c_new, n_new, m_pad[:, 0:C_HEADS, 0])


def _layer_sample(x, st, p, c):
    n = x.shape[0]
    x2d = x.reshape(n, D_MODEL)
    z_a, z_pool, z_c, z_if, z_g = _proj(x2d, p["norm1_g"], p["w_all"])
    ya, k_buf, v_buf = _swa_sample(z_a, st["k"], st["v"], c["bias_s"], p["sink"], p["qg64"], p["kg"],
                                   c["bd"][0:A_KV_COLS, 0:A_KV_COLS])
    yb = _pool_sample(z_pool, st["pool"], p["w_pool"], p["pool_scale"])
    hc, c_new, n_new, m_new = _mlstm_sample(z_c[:, 0:3 * C_W], z_if, p["gate_bias"], st["c"], st["n"], st["m"])
    x2d = _merge(x2d, ya, yb, hc, z_c, z_g, p["hg"], p["wa"], p["wb"], p["wc"], p["wo"])
    x2d = _peer(x2d, p)
    pool_buf = jnp.concatenate([st["pool"][:, 1:, :], z_pool[:, None, :]], axis=1)
    return x2d.reshape(n, 1, D_MODEL), (k_buf, v_buf, pool_buf, c_new, n_new, m_new)


def kernel(x_prompt, x_sample, cache_k, cache_v, state_pool, state_mlstm_c, state_mlstm_n, state_mlstm_m, rel_bias, norm1_g, w_in, q_norm_g, k_norm_g, attn_sink, w_pool, pool_scale, b_igate, b_fgate, h_norm_g, w_br_a, w_br_b, w_br_c, w_out, norm2_g, peer_wq, peer_k1, peer_k2, peer_u, peer_v):
    a = dict(norm1_g=norm1_g, w_in=w_in, q_norm_g=q_norm_g, k_norm_g=k_norm_g, attn_sink=attn_sink, w_pool=w_pool,
             pool_scale=pool_scale, b_igate=b_igate, b_fgate=b_fgate, h_norm_g=h_norm_g, w_br_a=w_br_a,
             w_br_b=w_br_b, w_br_c=w_br_c, w_out=w_out, norm2_g=norm2_g, peer_wq=peer_wq, peer_k1=peer_k1,
             peer_k2=peer_k2, peer_u=peer_u, peer_v=peer_v)
    depth = w_in.shape[0]
    bd, tri, sel = _constants()
    bias_p, bias_s = _bias_tables(rel_bias)
    consts = dict(bd=bd, tri=tri, bias_p=bias_p, bias_s=bias_s)
    xp, xs = x_prompt, x_sample
    new_p = [[] for _ in range(6)]
    new_s = [[] for _ in range(6)]
    for l in range(depth):
        p = _layer_params(l, a)
        p["sel"] = sel
        p["selt"] = sel.T
        st_s = dict(k=cache_k[l], v=cache_v[l], pool=state_pool[l], c=state_mlstm_c[l], n=state_mlstm_n[l],
                    m=state_mlstm_m[l])
        xp, sp = _layer_prompt(xp, p, consts)
        xs, ss = _layer_sample(xs, st_s, p, consts)
        for i in range(6):
            new_p[i].append(sp[i])
            new_s[i].append(ss[i])
    outs_p = [jnp.stack(v) for v in new_p]
    outs_s = [jnp.stack(v) for v in new_s]
    return (xp, xs, *outs_p, *outs_s)
```

```python
import functools
import math

import jax
import jax.numpy as jnp
import numpy as np
from jax import lax
from jax.experimental import pallas as pl
from jax.experimental.pallas import tpu as pltpu

f32 = jnp.float32
bf16 = jnp.bfloat16
i32 = jnp.int32

D_MODEL = 1024
A_HEADS = 8
A_KV_HEADS = 2
A_HEAD_DIM = 64
A_GROUP = A_HEADS // A_KV_HEADS
WINDOW = 128
N_BUCKETS = 32
MAX_DISTANCE = 128
A_Q_COLS = A_HEADS * A_HEAD_DIM
A_KV_COLS = A_KV_HEADS * A_HEAD_DIM
POOL_WINDOWS = (2, 4, 8, 16)
POOL_GROUP_W = 128
POOL_W = 512
POOL_BUF = 15
C_HEADS = 4
C_HEAD_DIM = 128
C_W = C_HEADS * C_HEAD_DIM
C_CHUNK = 128
PAST_LEN = 16384
PEER_HEADS = 8
PEER_DK = 256
N_KEYS = 128
N_EXPERTS = N_KEYS * N_KEYS
PEER_TOPK = 16
PEER_PICKS = PEER_HEADS * PEER_TOPK
EPS = 1e-6
NEG_INF = float("-inf")

LANES = 128
ROW_CHUNKS = D_MODEL // LANES
HALF_CHUNKS = ROW_CHUNKS // 2
MIB = 1024 * 1024
PEER_TB = 64
TOK_UNROLL = 8
GROUP_TOK = 2 * TOK_UNROLL
GROUP_IDX = GROUP_TOK * PEER_PICKS

PROJ_SPLITS = (A_Q_COLS + 2 * A_KV_COLS, POOL_W, 4 * C_W, LANES, 3 * D_MODEL)

STAIR_ROWS = tuple(PEER_TOPK // (a + 1) for a in range(PEER_TOPK))
STAIR_N = sum(STAIR_ROWS)
STAIR_PAD = (-STAIR_N) % 8


def _cparams(sem, vmem_mib):
    return pltpu.CompilerParams(dimension_semantics=sem, vmem_limit_bytes=vmem_mib * MIB)


def _resident(shape):
    nd = len(shape)
    return pl.BlockSpec(shape, lambda *_: (0,) * nd, pipeline_mode=pl.Buffered(1))


def _dot(a, b):
    return jnp.dot(a, b, preferred_element_type=f32)


def _dot_nt(a, b):
    return lax.dot_general(a, b, (((1,), (1,)), ((), ())), preferred_element_type=f32)


def _dot_tn(a, b):
    return lax.dot_general(a, b, (((0,), (0,)), ((), ())), preferred_element_type=f32)


def _split_bf16(x):
    hi = x.astype(bf16)
    lo = (x - hi.astype(f32)).astype(bf16)
    return hi, lo


def _sigmoid(x):
    return 1.0 / (1.0 + jnp.exp(-x))


def _log_sigmoid(x):
    return jnp.minimum(x, 0.0) - jnp.log(1.0 + jnp.exp(-jnp.abs(x)))


def _rms_rows(x, g):
    ms = jnp.mean(x * x, axis=-1, keepdims=True)
    return x * lax.rsqrt(ms + EPS) * g


def _headnorm(x, blockdiag, g, width):
    hi, lo = _split_bf16(x * x)
    ss = _dot(hi, blockdiag) + _dot(lo, blockdiag)
    return x * lax.rsqrt(ss * (1.0 / width) + EPS) * g


def _proj_kernel(x_ref, g_ref, w_ref, *o_refs):
    xn = _rms_rows(x_ref[...], g_ref[...]).astype(bf16)
    off = 0
    for o_ref, width in zip(o_refs, PROJ_SPLITS):
        o_ref[...] = _dot(xn, w_ref[:, off:off + width])
        off += width


def _proj(x2d, g, w_all):
    m = x2d.shape[0]
    tm = min(256, m)
    total = sum(PROJ_SPLITS)
    return pl.pallas_call(
        _proj_kernel,
        grid=(m // tm,),
        in_specs=[pl.BlockSpec((tm, D_MODEL), lambda i: (i, 0)),
                  _resident((1, D_MODEL)),
                  _resident((D_MODEL, total))],
        out_specs=[pl.BlockSpec((tm, w), lambda i: (i, 0)) for w in PROJ_SPLITS],
        out_shape=[jax.ShapeDtypeStruct((m, w), f32) for w in PROJ_SPLITS],
        compiler_params=_cparams(("parallel",), 48),
        name="proj",
    )(x2d, g.reshape(1, D_MODEL), w_all)


def _swa_kernel(sink_ref, zc_ref, kp_ref, vp_ref, bias_ref, qg_ref, kg_ref, bd_ref, ya_ref, kn_ref):
    i = pl.program_id(1)
    zc = zc_ref[0]
    bd = bd_ref[...]
    bdk = bd[0:A_KV_COLS, 0:A_KV_COLS]
    qn = _headnorm(zc[:, 0:A_Q_COLS], bd, qg_ref[...], A_HEAD_DIM)
    kcn = _headnorm(zc[:, A_Q_COLS:A_Q_COLS + A_KV_COLS], bdk, kg_ref[...], A_HEAD_DIM)
    kpn = _headnorm(kp_ref[0], bdk, kg_ref[...], A_HEAD_DIM)
    kn_ref[0] = kcn
    vc = zc[:, A_Q_COLS + A_KV_COLS:A_Q_COLS + 2 * A_KV_COLS]
    kcat = jnp.concatenate([kpn, kcn], axis=0).astype(bf16)
    vcat = jnp.concatenate([vp_ref[0], vc], axis=0).astype(bf16)
    qb = qn.astype(bf16)
    col = lax.broadcasted_iota(i32, (WINDOW, 2 * WINDOW), 1)
    has_prev = jnp.logical_or(col >= WINDOW, i > 0)
    outs = []
    for h in range(A_HEADS):
        kv = h // A_GROUP
        ks = slice(kv * A_HEAD_DIM, (kv + 1) * A_HEAD_DIM)
        logits = _dot_nt(qb[:, h * A_HEAD_DIM:(h + 1) * A_HEAD_DIM], kcat[:, ks]) * (A_HEAD_DIM ** -0.5)
        logits = jnp.where(has_prev, logits + bias_ref[h], NEG_INF)
        sink = sink_ref[h]
        mx = jnp.maximum(jnp.max(logits, axis=-1, keepdims=True), sink)
        p = jnp.exp(logits - mx)
        p = p / (jnp.sum(p, axis=-1, keepdims=True) + jnp.exp(sink - mx))
        outs.append(_dot(p.astype(bf16), vcat[:, ks]))
    ya_ref[0] = jnp.concatenate(outs, axis=-1)


def _swa_prompt(z_a, bias_tab, sink, qg, kg, bd, n, t):
    nb = t // WINDOW
    z3 = z_a.reshape(n, t, PROJ_SPLITS[0])
    kblk = A_Q_COLS // A_KV_COLS
    prev = lambda b, i: (b, jnp.maximum(i - 1, 0), kblk)
    prev_v = lambda b, i: (b, jnp.maximum(i - 1, 0), kblk + 1)
    ya, kn = pl.pallas_call(
        _swa_kernel,
        grid=(n, nb),
        in_specs=[pl.BlockSpec(memory_space=pltpu.SMEM),
                  pl.BlockSpec((1, WINDOW, PROJ_SPLITS[0]), lambda b, i: (b, i, 0)),
                  pl.BlockSpec((1, WINDOW, A_KV_COLS), prev),
                  pl.BlockSpec((1, WINDOW, A_KV_COLS), prev_v),
                  _resident((A_HEADS, WINDOW, 2 * WINDOW)),
                  _resident((1, A_Q_COLS)),
                  _resident((1, A_KV_COLS)),
                  _resident((A_Q_COLS, A_Q_COLS))],
        out_specs=[pl.BlockSpec((1, WINDOW, A_Q_COLS), lambda b, i: (b, i, 0)),
                   pl.BlockSpec((1, WINDOW, A_KV_COLS), lambda b, i: (b, i, 0))],
        out_shape=[jax.ShapeDtypeStruct((n, t, A_Q_COLS), f32),
                   jax.ShapeDtypeStruct((n, t, A_KV_COLS), f32)],
        compiler_params=_cparams(("parallel", "arbitrary"), 32),
        name="swa_prompt",
    )(sink, z3, z3, z3, bias_tab, qg, kg, bd)
    return ya, kn


def _swa_sample_kernel(q_ref, kn_ref, vn_ref, ck_ref, cv_ref, bias_ref, sink_ref, qg_ref, kg_ref, bd_ref,
                       ya_ref, ko_ref, vo_ref):
    q = q_ref[...]
    qn = q * lax.rsqrt(jnp.mean(q * q, axis=-1, keepdims=True) + EPS) * qg_ref[...]
    kn = _headnorm(kn_ref[...], bd_ref[...], kg_ref[...], A_HEAD_DIM)
    rowi = lax.broadcasted_iota(i32, ck_ref.shape, 1)
    newest = rowi == WINDOW - 1
    k_all = jnp.where(newest, kn[:, None, :], pltpu.roll(ck_ref[...], WINDOW - 1, 1))
    v_all = jnp.where(newest, vn_ref[...][:, None, :], pltpu.roll(cv_ref[...], WINDOW - 1, 1))
    ko_ref[...] = k_all
    vo_ref[...] = v_all
    outs = []
    for kv in range(A_KV_HEADS):
        ks = slice(kv * A_HEAD_DIM, (kv + 1) * A_HEAD_DIM)
        hs = slice(kv * A_GROUP, (kv + 1) * A_GROUP)
        kh = k_all[:, :, ks].astype(bf16)
        vh = v_all[:, :, ks].astype(bf16)
        qh = qn[:, hs, :].astype(bf16)
        logits = jnp.einsum('bgd,bsd->bgs', qh, kh, preferred_element_type=f32) * (A_HEAD_DIM ** -0.5)
        logits = logits + bias_ref[hs, :][None]
        sink = sink_ref[hs, :][None]
        mx = jnp.maximum(jnp.max(logits, axis=-1, keepdims=True), sink)
        p = jnp.exp(logits - mx)
        p = p / (jnp.sum(p, axis=-1, keepdims=True) + jnp.exp(sink - mx))
        outs.append(jnp.einsum('bgs,bsd->bgd', p.astype(bf16), vh, preferred_element_type=f32))
    ya_ref[...] = jnp.concatenate(outs, axis=1)


def _swa_sample(z_a, cache_k, cache_v, bias_s, sink, qg64, kg, bdk):
    n = z_a.shape[0]
    bn = 8
    q3 = z_a[:, 0:A_Q_COLS].reshape(n, A_HEADS, A_HEAD_DIM)
    k_new = z_a[:, A_Q_COLS:A_Q_COLS + A_KV_COLS]
    v_new = z_a[:, A_Q_COLS + A_KV_COLS:]
    ck = cache_k.reshape(n, WINDOW, A_KV_COLS)
    cv = cache_v.reshape(n, WINDOW, A_KV_COLS)
    row2 = pl.BlockSpec((bn, A_KV_COLS), lambda i: (i, 0))
    cache = pl.BlockSpec((bn, WINDOW, A_KV_COLS), lambda i: (i, 0, 0))
    q_spec = pl.BlockSpec((bn, A_HEADS, A_HEAD_DIM), lambda i: (i, 0, 0))
    ya, ko, vo = pl.pallas_call(
        _swa_sample_kernel,
        grid=(n // bn,),
        in_specs=[q_spec, row2, row2, cache, cache,
                  _resident((A_HEADS, WINDOW)), _resident((A_HEADS, 1)),
                  _resident((1, A_HEAD_DIM)), _resident((1, A_KV_COLS)), _resident((A_KV_COLS, A_KV_COLS))],
        out_specs=[q_spec, cache, cache],
        out_shape=[jax.ShapeDtypeStruct((n, A_HEADS, A_HEAD_DIM), f32),
                   jax.ShapeDtypeStruct((n, WINDOW, A_KV_COLS), f32),
                   jax.ShapeDtypeStruct((n, WINDOW, A_KV_COLS), f32)],
        compiler_params=_cparams(("parallel",), 32),
        name="swa_sample",
    )(q3, k_new, v_new, ck, cv, bias_s, sink.reshape(A_HEADS, 1), qg64, kg, bdk)
    shape = (n, WINDOW, A_KV_HEADS, A_HEAD_DIM)
    return ya.reshape(n, A_Q_COLS), ko.reshape(shape), vo.reshape(shape)


def _pool_kernel(x_ref, w_ref, scale_ref, y_ref, buf_ref):
    j = pl.program_id(1)
    bt = x_ref.shape[1]
    halo = POOL_BUF + 1

    @pl.when(j == 0)
    def _():
        buf_ref[0:halo, :] = jnp.zeros((halo, POOL_W), f32)

    x = x_ref[0]
    buf_ref[halo:halo + bt, :] = x
    pos = j * bt + lax.broadcasted_iota(i32, (bt, 1), 0)
    outs = []
    for g, w in enumerate(POOL_WINDOWS):
        cs = slice(g * POOL_GROUP_W, (g + 1) * POOL_GROUP_W)
        xg = x[:, cs]
        s = xg
        for k in range(1, w):
            s = s + buf_ref[halo - k:halo - k + bt, cs]
        cnt = jnp.minimum(pos + 1, w).astype(f32)
        pooled = s / cnt - xg
        outs.append(_dot(pooled.astype(bf16), w_ref[g]))
    y_ref[0] = jnp.concatenate(outs, axis=-1) * scale_ref[...]
    buf_ref[0:halo, :] = x[bt - halo:bt, :]


def _pool_prompt(z_pool, w_pool, scale, n, t):
    bt = min(512, t)
    z3 = z_pool.reshape(n, t, POOL_W)
    spec = pl.BlockSpec((1, bt, POOL_W), lambda b, j: (b, j, 0))
    return pl.pallas_call(
        _pool_kernel,
        grid=(n, t // bt),
        in_specs=[spec, _resident((4, POOL_GROUP_W, POOL_GROUP_W)), _resident((1, POOL_W))],
        out_specs=spec,
        out_shape=jax.ShapeDtypeStruct((n, t, POOL_W), f32),
        scratch_shapes=[pltpu.VMEM((bt + POOL_BUF + 1, POOL_W), f32)],
        compiler_params=_cparams(("parallel", "arbitrary"), 32),
        name="pool_prompt",
    )(z3, w_pool, scale)


def _pool_sample_kernel(x_ref, st_ref, w_ref, scale_ref, y_ref):
    x = x_ref[...]
    outs = []
    for g, w in enumerate(POOL_WINDOWS):
        cs = slice(g * POOL_GROUP_W, (g + 1) * POOL_GROUP_W)
        xg = x[:, cs]
        s = xg
        for k in range(1, w):
            s = s + st_ref[:, POOL_BUF - k, cs]
        cnt = float(min(PAST_LEN + 1, w))
        pooled = s / cnt - xg
        outs.append(_dot(pooled.astype(bf16), w_ref[g]))
    y_ref[...] = jnp.concatenate(outs, axis=-1) * scale_ref[...]


def _pool_sample(z_pool, state, w_pool, scale):
    n = z_pool.shape[0]
    bn = 8
    return pl.pallas_call(
        _pool_sample_kernel,
        grid=(n // bn,),
        in_specs=[pl.BlockSpec((bn, POOL_W), lambda i: (i, 0)),
                  pl.BlockSpec((bn, POOL_BUF, POOL_W), lambda i: (i, 0, 0)),
                  _resident((4, POOL_GROUP_W, POOL_GROUP_W)), _resident((1, POOL_W))],
        out_specs=pl.BlockSpec((bn, POOL_W), lambda i: (i, 0)),
        out_shape=jax.ShapeDtypeStruct((n, POOL_W), f32),
        compiler_params=_cparams(("parallel",), 32),
        name="pool_sample",
    )(z_pool, state, w_pool, scale)


def _mlstm_kernel(zc_ref, zif_ref, gb_ref, tri_ref, h_ref, c_out, n_out, m_out, c_s, n_s, m_s):
    j = pl.program_id(1)
    L = C_CHUNK

    @pl.when(j == 0)
    def _():
        c_s[...] = jnp.zeros(c_s.shape, f32)
        n_s[...] = jnp.zeros(n_s.shape, f32)
        m_s[...] = jnp.zeros(m_s.shape, f32)

    zc = zc_ref[0]
    gates = zif_ref[0] + gb_ref[...]
    lf_hi, lf_lo = _split_bf16(_log_sigmoid(gates))
    bcum = _dot(tri_ref[...], lf_hi) + _dot(tri_ref[...], lf_lo)
    bcum_t = bcum.T
    gates_t = gates.T
    row = lax.broadcasted_iota(i32, (L, L), 0)
    col = lax.broadcasted_iota(i32, (L, L), 1)
    causal = col <= row
    for h in range(C_HEADS):
        hs = slice(h * C_HEAD_DIM, (h + 1) * C_HEAD_DIM)
        q = zc[:, hs]
        k = zc[:, C_W + h * C_HEAD_DIM:C_W + (h + 1) * C_HEAD_DIM] * (C_HEAD_DIM ** -0.5)
        v = zc[:, 2 * C_W + h * C_HEAD_DIM:2 * C_W + (h + 1) * C_HEAD_DIM]
        b_col = bcum[:, C_HEADS + h:C_HEADS + h + 1]
        b_row = bcum_t[C_HEADS + h:C_HEADS + h + 1, :]
        ig_col = gates[:, h:h + 1]
        ig_row = gates_t[h:h + 1, :]
        m_prev = m_s[h:h + 1, 0:1]
        dmat = jnp.where(causal, b_col - b_row + ig_row, NEG_INF)
        inter = b_col + m_prev
        m_t = jnp.maximum(inter, jnp.max(dmat, axis=-1, keepdims=True))
        qb, kb, vb = q.astype(bf16), k.astype(bf16), v.astype(bf16)
        s = _dot_nt(qb, kb) * jnp.exp(dmat - m_t)
        w_inter = jnp.exp(inter - m_t)
        c_prev = c_s[h]
        n_prev = n_s[h:h + 1, :]
        num = _dot(s.astype(bf16), vb) + w_inter * _dot_nt(qb, c_prev.astype(bf16))
        den = jnp.sum(s, axis=-1, keepdims=True) + w_inter * jnp.sum(q * n_prev, axis=-1, keepdims=True)
        h_ref[0, :, hs] = num / jnp.maximum(jnp.abs(den), jnp.exp(-m_t))
        m_new = m_t[L - 1:L, :]
        b_last = b_col[L - 1:L, :]
        w_end = jnp.exp(b_last - b_col + ig_col - m_new)
        decay = jnp.exp(b_last + m_prev - m_new)
        c_s[h] = decay * c_prev + _dot_tn((v * w_end).astype(bf16), kb)
        n_s[h:h + 1, :] = decay * n_prev + jnp.sum(k * w_end, axis=0, keepdims=True)
        m_s[h:h + 1, :] = jnp.broadcast_to(m_new, (1, LANES))

    @pl.when(j == pl.num_programs(1) - 1)
    def _():
        c_out[0] = c_s[...]
        n_out[0] = n_s[0:C_HEADS, :]
        m_out[0] = m_s[...]


def _mlstm_prompt(z_c, z_if, gate_bias, tri, n, t):
    nc = t // C_CHUNK
    zc3 = z_c.reshape(n, t, 4 * C_W)
    zif3 = z_if.reshape(n, t, LANES)
    return pl.pallas_call(
        _mlstm_kernel,
        grid=(n, nc),
        in_specs=[pl.BlockSpec((1, C_CHUNK, 3 * C_W), lambda b, j: (b, j, 0)),
                  pl.BlockSpec((1, C_CHUNK, LANES), lambda b, j: (b, j, 0)),
                  _resident((1, LANES)),
                  _resident((C_CHUNK, C_CHUNK))],
        out_specs=[pl.BlockSpec((1, C_CHUNK, C_W), lambda b, j: (b, j, 0)),
                   pl.BlockSpec((1, C_HEADS, C_HEAD_DIM, C_HEAD_DIM), lambda b, j: (b, 0, 0, 0)),
                   pl.BlockSpec((1, C_HEADS, C_HEAD_DIM), lambda b, j: (b, 0, 0)),
                   pl.BlockSpec((1, 8, LANES), lambda b, j: (b, 0, 0))],
        out_shape=[jax.ShapeDtypeStruct((n, t, C_W), f32),
                   jax.ShapeDtypeStruct((n, C_HEADS, C_HEAD_DIM, C_HEAD_DIM), f32),
                   jax.ShapeDtypeStruct((n, C_HEADS, C_HEAD_DIM), f32),
                   jax.ShapeDtypeStruct((n, 8, LANES), f32)],
        scratch_shapes=[pltpu.VMEM((C_HEADS, C_HEAD_DIM, C_HEAD_DIM), f32),
                        pltpu.VMEM((8, LANES), f32),
                        pltpu.VMEM((8, LANES), f32)],
        compiler_params=_cparams(("parallel", "arbitrary"), 32),
        name="mlstm_prompt",
    )(zc3, zif3, gate_bias, tri)


def _mlstm_sample_kernel(zc_ref, zif_ref, gb_ref, c_ref, n_ref, m_ref, h_ref, co_ref, no_ref, mo_ref):
    bn = zc_ref.shape[0]
    zc = zc_ref[...]
    gates = zif_ref[...] + gb_ref[...]
    lfs = _log_sigmoid(gates)
    lane = lax.broadcasted_iota(i32, (C_HEAD_DIM, LANES), 1)
    pad = jnp.zeros((LANES - bn, C_HEAD_DIM), f32)
    for h in range(C_HEADS):
        hs = slice(h * C_HEAD_DIM, (h + 1) * C_HEAD_DIM)
        q = zc[:, hs]
        k = zc[:, C_W + h * C_HEAD_DIM:C_W + (h + 1) * C_HEAD_DIM] * (C_HEAD_DIM ** -0.5)
        v = zc[:, 2 * C_W + h * C_HEAD_DIM:2 * C_W + (h + 1) * C_HEAD_DIM]
        ig = gates[:, h:h + 1]
        lf = lfs[:, C_HEADS + h:C_HEADS + h + 1]
        m_prev = m_ref[:, h:h + 1]
        n_prev = n_ref[:, h, :]
        inter = lf + m_prev
        m_t = jnp.maximum(inter, ig)
        s = jnp.sum(q * k, axis=-1, keepdims=True) * jnp.exp(ig - m_t)
        w_inter = jnp.exp(inter - m_t)
        w_end = jnp.exp(ig - m_t)
        cq_t = jnp.zeros((C_HEAD_DIM, LANES), f32)
        for b in range(bn):
            colsum = jnp.sum(c_ref[b, h] * q[b:b + 1, :], axis=-1, keepdims=True)
            cq_t = jnp.where(lane == b, colsum, cq_t)
        cq = cq_t.T[0:bn, :]
        num = s * v + w_inter * cq
        den = s + w_inter * jnp.sum(n_prev * q, axis=-1, keepdims=True)
        h_ref[:, hs] = num / jnp.maximum(jnp.abs(den), jnp.exp(-m_t))
        vw_t = jnp.concatenate([v * w_end, pad], axis=0).T
        for b in range(bn):
            co_ref[b, h] = w_inter[b:b + 1, :] * c_ref[b, h] + vw_t[:, b:b + 1] * k[b:b + 1, :]
        no_ref[:, h, :] = w_inter * n_prev + w_end * k
        mo_ref[:, h:h + 1] = m_t


def _mlstm_sample(z_c, z_if, gate_bias, c0, n0, m0):
    n = z_c.shape[0]
    bn = 8
    c_spec = pl.BlockSpec((bn, C_HEADS, C_HEAD_DIM, C_HEAD_DIM), lambda i: (i, 0, 0, 0))
    n_spec = pl.BlockSpec((bn, C_HEADS, C_HEAD_DIM), lambda i: (i, 0, 0))
    m_spec = pl.BlockSpec((bn, C_HEADS), lambda i: (i, 0))
    return pl.pallas_call(
        _mlstm_sample_kernel,
        grid=(n // bn,),
        in_specs=[pl.BlockSpec((bn, 3 * C_W), lambda i: (i, 0)),
                  pl.BlockSpec((bn, LANES), lambda i: (i, 0)),
                  _resident((1, LANES)), c_spec, n_spec, m_spec],
        out_specs=[pl.BlockSpec((bn, C_W), lambda i: (i, 0)), c_spec, n_spec, m_spec],
        out_shape=[jax.ShapeDtypeStruct((n, C_W), f32),
                   jax.ShapeDtypeStruct(c0.shape, f32),
                   jax.ShapeDtypeStruct(n0.shape, f32),
                   jax.ShapeDtypeStruct(m0.shape, f32)],
        compiler_params=_cparams(("parallel",), 32),
        name="mlstm_sample",
    )(z_c, z_if, gate_bias, c0, n0, m0)


def _merge_kernel(x_ref, ya_ref, yb_ref, h_ref, co_ref, zg_ref, hg_ref, wa_ref, wb_ref, wc_ref, wo_ref, o_ref):
    h = h_ref[...]
    parts = []
    for hh in range(C_HEADS):
        hs = h[:, hh * C_HEAD_DIM:(hh + 1) * C_HEAD_DIM]
        parts.append(hs * lax.rsqrt(jnp.mean(hs * hs, axis=-1, keepdims=True) + EPS))
    yc = jnp.concatenate(parts, axis=-1) * hg_ref[...] * _sigmoid(co_ref[...])
    zg = zg_ref[...]
    merged = (_sigmoid(zg[:, 0:D_MODEL]) * _dot(ya_ref[...].astype(bf16), wa_ref[...])
              + _sigmoid(zg[:, D_MODEL:2 * D_MODEL]) * _dot(yb_ref[...].astype(bf16), wb_ref[...])
              + _sigmoid(zg[:, 2 * D_MODEL:]) * _dot(yc.astype(bf16), wc_ref[...]))
    o_ref[...] = x_ref[...] + _dot(merged.astype(bf16), wo_ref[...])


def _merge(x2d, ya, yb, hc, z_c, z_g, hg, wa, wb, wc, wo):
    m = x2d.shape[0]
    tb = min(256, m)
    row = lambda w: pl.BlockSpec((tb, w), lambda i: (i, 0))
    return pl.pallas_call(
        _merge_kernel,
        grid=(m // tb,),
        in_specs=[row(D_MODEL), row(A_Q_COLS), row(POOL_W), row(C_W),
                  pl.BlockSpec((tb, C_W), lambda i: (i, 3)),
                  row(3 * D_MODEL), _resident((1, C_W)),
                  _resident((A_Q_COLS, D_MODEL)), _resident((POOL_W, D_MODEL)), _resident((C_W, D_MODEL)),
                  _resident((D_MODEL, D_MODEL))],
        out_specs=row(D_MODEL),
        out_shape=jax.ShapeDtypeStruct((m, D_MODEL), f32),
        compiler_params=_cparams(("parallel",), 48),
        name="merge",
    )(x2d, ya, yb, hc, z_c, z_g, hg, wa, wb, wc, wo)


def _topk_rows(s, k):
    nrows = s.shape[0]
    riota = lax.broadcasted_iota(i32, s.shape, 0).astype(f32)
    vals, idxs = [], []
    for _ in range(k):
        m = jnp.max(s, axis=0, keepdims=True)
        i = jnp.min(jnp.where(s == m, riota, float(nrows)), axis=0, keepdims=True)
        vals.append(m)
        idxs.append(i)
        s = jnp.where(riota == i, NEG_INF, s)
    return vals, idxs


def _pack_pairs(x):
    bits = pltpu.bitcast(x, i32)
    r = bits + (jnp.int32(0x7FFF) + (lax.shift_right_logical(bits, 16) & 1))
    return [lax.shift_right_logical(r[:, 2 * s * LANES:(2 * s + 1) * LANES], 16)
            | (r[:, (2 * s + 1) * LANES:(2 * s + 2) * LANES] & jnp.int32(-65536))
            for s in range(HALF_CHUNKS)]


def _route_kernel(x_ref, g_ref, wq_ref, k1_ref, k2_ref, idx_ref, gate_ref, xp_ref):
    tb = x_ref.shape[0]
    xn = _rms_rows(x_ref[...], g_ref[...])
    for s, words in enumerate(_pack_pairs(xn)):
        xp_ref[:, s, :] = words
    qb = _dot(xn.astype(bf16), wq_ref[...]).astype(bf16)
    half = PEER_DK // 2
    exp_rows, gate_rows = [], []
    for h in range(PEER_HEADS):
        s1 = _dot_nt(k1_ref[...], qb[:, h * PEER_DK:h * PEER_DK + half])
        s2 = _dot_nt(k2_ref[...], qb[:, h * PEER_DK + half:(h + 1) * PEER_DK])
        v1, i1 = _topk_rows(s1, PEER_TOPK)
        v2, i2 = _topk_rows(s2, PEER_TOPK)
        v2s = jnp.concatenate(v2, axis=0)
        i2s = jnp.concatenate(i2, axis=0)
        cv, ce = [], []
        for a, nb in enumerate(STAIR_ROWS):
            cv.append(v1[a] + v2s[0:nb])
            ce.append((i1[a] * float(N_KEYS) + i2s[0:nb]) * float(HALF_CHUNKS))
        if STAIR_PAD:
            cv.append(jnp.full((STAIR_PAD, tb), NEG_INF, f32))
            ce.append(jnp.zeros((STAIR_PAD, tb), f32))
        cand = jnp.concatenate(cv, axis=0)
        cexp = jnp.concatenate(ce, axis=0)
        riota = lax.broadcasted_iota(i32, cand.shape, 0).astype(f32)
        tv, ti = _topk_rows(cand, PEER_TOPK)
        te = [jnp.max(jnp.where(riota == i, cexp, -1.0), axis=0, keepdims=True) for i in ti]
        tvs = jnp.concatenate(tv, axis=0)
        ex = jnp.exp(tvs - tv[0])
        gate_rows.append(ex / jnp.sum(ex, axis=0, keepdims=True))
        exp_rows.append(jnp.concatenate(te, axis=0))
    idx_ref[...] = jnp.concatenate(exp_rows, axis=0).T.astype(i32)
    gate_ref[...] = jnp.concatenate(gate_rows, axis=0).T


def _route(x2d, g, wq, k1, k2):
    m = x2d.shape[0]
    tb = min(256, m)
    return pl.pallas_call(
        _route_kernel,
        grid=(m // tb,),
        in_specs=[pl.BlockSpec((tb, D_MODEL), lambda i: (i, 0)),
                  _resident((1, D_MODEL)),
                  _resident((D_MODEL, PEER_HEADS * PEER_DK)),
                  _resident((N_KEYS, PEER_DK // 2)), _resident((N_KEYS, PEER_DK // 2))],
        out_specs=[pl.BlockSpec((tb, PEER_PICKS), lambda i: (i, 0)),
                   pl.BlockSpec((tb, PEER_PICKS), lambda i: (i, 0)),
                   pl.BlockSpec((tb, HALF_CHUNKS, LANES), lambda i: (i, 0, 0))],
        out_shape=[jax.ShapeDtypeStruct((m, PEER_PICKS), i32),
                   jax.ShapeDtypeStruct((m, PEER_PICKS), f32),
                   jax.ShapeDtypeStruct((m, HALF_CHUNKS, LANES), i32)],
        compiler_params=_cparams(("parallel",), 48),
        name="peer_route",
    )(x2d, g.reshape(1, D_MODEL), wq, k1, k2)


def _pack_table_kernel(w_ref, o_ref):
    rows = w_ref.shape[0]
    for s, words in enumerate(_pack_pairs(w_ref[...])):
        o_ref[pl.ds(s, rows, stride=HALF_CHUNKS), :] = words


def _pack_table(w):
    e = w.shape[0]
    rows = 512
    return pl.pallas_call(
        _pack_table_kernel,
        grid=(e // rows,),
        in_specs=[pl.BlockSpec((rows, D_MODEL), lambda i: (i, 0))],
        out_specs=pl.BlockSpec((rows * HALF_CHUNKS, LANES), lambda i: (i, 0)),
        out_shape=jax.ShapeDtypeStruct((e * HALF_CHUNKS, LANES), i32),
        compiler_params=_cparams(("parallel",), 32),
        name="pack_table",
    )(w)


def _gather_groups(idx_hbm, idx_bufs, sem, tb, group_fn):
    assert tb % (2 * GROUP_TOK) == 0, tb
    ngroups = tb // GROUP_TOK
    npairs = ngroups // 2
    first = pl.program_id(0) * ngroups

    def copy(grp, slot):
        src = idx_hbm.at[pl.ds((first + grp) * GROUP_IDX, GROUP_IDX)]
        return pltpu.make_async_copy(src, idx_bufs[slot], sem.at[slot])

    copy(0, 0).start()

    def pair(i2, carry):
        copy(2 * i2 + 1, 1).start()
        copy(2 * i2, 0).wait()
        group_fn(pl.multiple_of(2 * i2 * GROUP_TOK, GROUP_TOK), idx_bufs[0])

        @pl.when(i2 + 1 < npairs)
        def _():
            copy(2 * i2 + 2, 0).start()

        copy(2 * i2 + 1, 1).wait()
        group_fn(pl.multiple_of((2 * i2 + 1) * GROUP_TOK, GROUP_TOK), idx_bufs[1])
        return carry

    lax.fori_loop(0, npairs, pair, 0)


def _gather_rows(tab_ref, idx_smem, c, g_ref):
    for j in range(PEER_PICKS):
        off = pl.multiple_of(idx_smem[c * PEER_PICKS + j], HALF_CHUNKS)
        g_ref[HALF_CHUNKS * j:HALF_CHUNKS * (j + 1), :] = tab_ref[pl.ds(off, HALF_CHUNKS), :]


def _chunk_diag():
    lane = lax.broadcasted_iota(i32, (ROW_CHUNKS, ROW_CHUNKS * PEER_PICKS), 1)
    sub = lax.broadcasted_iota(i32, (ROW_CHUNKS, ROW_CHUNKS * PEER_PICKS), 0)
    return (lane & (ROW_CHUNKS - 1)) == sub, sub


def _peer_u_kernel(idx_hbm, x_ref, gate_ref, selt_ref, tab_ref, w_ref, g0_ref, g1_ref, s0_ref, s1_ref, sem):
    diag, sub = _chunk_diag()
    gbufs = (g0_ref, g1_ref)

    def group(tg, idx_smem):
        for tile in range(GROUP_TOK // TOK_UNROLL):
            t0 = pl.multiple_of(tg + tile * TOK_UNROLL, TOK_UNROLL)
            acc = jnp.zeros((TOK_UNROLL, ROW_CHUNKS * PEER_PICKS), f32)
            for c in range(TOK_UNROLL):
                g_ref = gbufs[c % 2]
                _gather_rows(tab_ref, idx_smem, tile * TOK_UNROLL + c, g_ref)
                xb = pltpu.bitcast(x_ref[t0 + c], bf16)
                out = _dot_nt(xb, pltpu.bitcast(g_ref[...], bf16))
                rowsum = jnp.sum(jnp.where(diag, out, 0.0), axis=0, keepdims=True)
                acc = jnp.where(sub == c, rowsum, acc)
            hi, lo = _split_bf16(acc)
            a = _dot(hi, selt_ref[...]) + _dot(lo, selt_ref[...])
            gelu = 0.5 * a * (1.0 + lax.erf(a * (2.0 ** -0.5)))
            w_ref[pl.ds(t0, TOK_UNROLL), :] = gate_ref[pl.ds(t0, TOK_UNROLL), :] * gelu

    _gather_groups(idx_hbm, (s0_ref, s1_ref), sem, x_ref.shape[0], group)


def _peer_scratch():
    return [pltpu.VMEM((HALF_CHUNKS * PEER_PICKS, LANES), i32),
            pltpu.VMEM((HALF_CHUNKS * PEER_PICKS, LANES), i32),
            pltpu.SMEM((GROUP_IDX,), i32),
            pltpu.SMEM((GROUP_IDX,), i32),
            pltpu.SemaphoreType.DMA((2,))]


def _peer_u(idx_flat, xp, gates, selt, tab):
    m = xp.shape[0]
    tb = min(PEER_TB, m)
    return pl.pallas_call(
        _peer_u_kernel,
        grid=(m // tb,),
        in_specs=[pl.BlockSpec(memory_space=pl.ANY),
                  pl.BlockSpec((tb, HALF_CHUNKS, LANES), lambda i: (i, 0, 0)),
                  pl.BlockSpec((tb, PEER_PICKS), lambda i: (i, 0)),
                  _resident((ROW_CHUNKS * PEER_PICKS, PEER_PICKS)),
                  _resident((N_EXPERTS * HALF_CHUNKS, LANES))],
        out_specs=pl.BlockSpec((tb, PEER_PICKS), lambda i: (i, 0)),
        out_shape=jax.ShapeDtypeStruct((m, PEER_PICKS), f32),
        scratch_shapes=_peer_scratch(),
        compiler_params=_cparams(("arbitrary",), 44),
        name="peer_u",
    )(idx_flat, xp, gates, selt, tab)


def _peer_v_kernel(idx_hbm, w_ref, x_ref, sel_ref, tab_ref, o_ref, g0_ref, g1_ref, s0_ref, s1_ref, sem):
    diag, _ = _chunk_diag()
    gbufs = (g0_ref, g1_ref)

    def group(tg, idx_smem):
        for tile in range(GROUP_TOK // TOK_UNROLL):
            t0 = pl.multiple_of(tg + tile * TOK_UNROLL, TOK_UNROLL)
            w8 = _dot(w_ref[pl.ds(t0, TOK_UNROLL), :].astype(bf16), sel_ref[...])
            rows = [jnp.zeros((TOK_UNROLL, LANES), f32) for _ in range(ROW_CHUNKS)]
            for c in range(TOK_UNROLL):
                g_ref = gbufs[c % 2]
                _gather_rows(tab_ref, idx_smem, tile * TOK_UNROLL + c, g_ref)
                wsel = jnp.where(diag, w8[c:c + 1, :], 0.0).astype(bf16)
                y = _dot(wsel, pltpu.bitcast(g_ref[...], bf16))
                rows = [jnp.where(tok == c, y[r:r + 1, :], rows[r]) for r in range(ROW_CHUNKS)]
            o_ref[pl.ds(t0, TOK_UNROLL), :] = x_ref[pl.ds(t0, TOK_UNROLL), :] + jnp.concatenate(rows, axis=-1)

    tok = lax.broadcasted_iota(i32, (TOK_UNROLL, LANES), 0)
    _gather_groups(idx_hbm, (s0_ref, s1_ref), sem, x_ref.shape[0], group)


def _peer_v(idx_flat, w2d, x2d, sel, tab):
    m = x2d.shape[0]
    tb = min(PEER_TB, m)
    return pl.pallas_call(
        _peer_v_kernel,
        grid=(m // tb,),
        in_specs=[pl.BlockSpec(memory_space=pl.ANY),
                  pl.BlockSpec((tb, PEER_PICKS), lambda i: (i, 0)),
                  pl.BlockSpec((tb, D_MODEL), lambda i: (i, 0)),
                  _resident((PEER_PICKS, ROW_CHUNKS * PEER_PICKS)),
                  _resident((N_EXPERTS * HALF_CHUNKS, LANES))],
        out_specs=pl.BlockSpec((tb, D_MODEL), lambda i: (i, 0)),
        out_shape=jax.ShapeDtypeStruct((m, D_MODEL), f32),
        scratch_shapes=_peer_scratch(),
        compiler_params=_cparams(("arbitrary",), 44),
        name="peer_v",
    )(idx_flat, w2d, x2d, sel, tab)


def _peer(x2d, p):
    m = x2d.shape[0]
    idx, gates, xp = _route(x2d, p["norm2_g"], p["wq"], p["k1"], p["k2"])
    idx_flat = idx.reshape(m * PEER_PICKS)
    w2d = _peer_u(idx_flat, xp, gates, p["selt"], p["u_tab"])
    return _peer_v(idx_flat, w2d, x2d, p["sel"], p["v_tab"])


def _rel_bucket(dist):
    max_exact = N_BUCKETS // 2
    d = jnp.maximum(dist, 0)
    df = jnp.maximum(d, 1).astype(f32)
    large = max_exact + (jnp.log(df / max_exact) / math.log(MAX_DISTANCE / max_exact)
                         * (N_BUCKETS - max_exact)).astype(i32)
    large = jnp.minimum(large, N_BUCKETS - 1)
    return jnp.where(d < max_exact, d, large)


def _bias_tables(rel_bias):
    by_dist = rel_bias[_rel_bucket(jnp.arange(2 * WINDOW))].astype(f32).T
    dist = np.arange(WINDOW)[:, None] + WINDOW - np.arange(2 * WINDOW)[None, :]
    valid = (dist >= 0) & (dist < WINDOW)
    padded = jnp.pad(by_dist, ((0, 0), (WINDOW - 1, 0)))
    rows = [padded[:, q:q + 2 * WINDOW][:, ::-1] for q in range(WINDOW)]
    bias_p = jnp.where(jnp.asarray(valid)[None], jnp.stack(rows, axis=1), NEG_INF)
    bias_s = by_dist[:, 0:WINDOW][:, ::-1]
    return bias_p, bias_s


def _layer_params(l, a):
    w = a["w_in"][l]
    o_pin = A_Q_COLS + 2 * A_KV_COLS
    o_c = o_pin + POOL_W
    o_if = o_c + 3 * C_W
    o_co = o_if + 2 * C_HEADS
    o_g = o_co + C_W
    w_if = jnp.pad(w[:, o_if:o_co], ((0, 0), (0, LANES - 2 * C_HEADS)))
    w_all = jnp.concatenate([w[:, 0:o_pin], w[:, o_pin:o_c], w[:, o_c:o_if], w[:, o_co:o_g], w_if, w[:, o_g:]],
                            axis=1).astype(bf16)
    gate_bias = jnp.pad(jnp.concatenate([a["b_igate"][l], a["b_fgate"][l]]), (0, LANES - 2 * C_HEADS))
    return dict(
        norm1_g=a["norm1_g"][l], w_all=w_all,
        qg=jnp.tile(a["q_norm_g"][l], A_HEADS).reshape(1, A_Q_COLS),
        qg64=a["q_norm_g"][l].reshape(1, A_HEAD_DIM),
        kg=jnp.tile(a["k_norm_g"][l], A_KV_HEADS).reshape(1, A_KV_COLS),
        sink=a["attn_sink"][l].astype(f32),
        w_pool=a["w_pool"][l].astype(bf16), pool_scale=a["pool_scale"][l].reshape(1, POOL_W),
        gate_bias=gate_bias.reshape(1, LANES).astype(f32),
        hg=jnp.tile(a["h_norm_g"][l], C_HEADS).reshape(1, C_W),
        wa=a["w_br_a"][l].astype(bf16), wb=a["w_br_b"][l].astype(bf16), wc=a["w_br_c"][l].astype(bf16),
        wo=a["w_out"][l].astype(bf16),
        norm2_g=a["norm2_g"][l], wq=a["peer_wq"][l].astype(bf16),
        k1=a["peer_k1"][l].astype(bf16), k2=a["peer_k2"][l].astype(bf16),
        u_tab=_pack_table(a["peer_u"][l]), v_tab=_pack_table(a["peer_v"][l]),
    )


def _constants():
    head = np.arange(A_Q_COLS) // A_HEAD_DIM
    bd = jnp.asarray(head[:, None] == head[None, :], bf16)
    tri = jnp.asarray(np.tril(np.ones((C_CHUNK, C_CHUNK))), bf16)
    sel = jnp.asarray(np.arange(ROW_CHUNKS * PEER_PICKS)[None, :] // ROW_CHUNKS
                      == np.arange(PEER_PICKS)[:, None], bf16)
    return bd, tri, sel


def _layer_prompt(x, p, c):
    n, t, _ = x.shape
    x2d = x.reshape(n * t, D_MODEL)
    z_a, z_pool, z_c, z_if, z_g = _proj(x2d, p["norm1_g"], p["w_all"])
    ya, kn = _swa_prompt(z_a, c["bias_p"], p["sink"], p["qg"], p["kg"], c["bd"], n, t)
    yb = _pool_prompt(z_pool, p["w_pool"], p["pool_scale"], n, t)
    hc, c_new, n_new, m_pad = _mlstm_prompt(z_c, z_if, p["gate_bias"], c["tri"], n, t)
    x2d = _merge(x2d, ya.reshape(n * t, A_Q_COLS), yb.reshape(n * t, POOL_W), hc.reshape(n * t, C_W),
                 z_c, z_g, p["hg"], p["wa"], p["wb"], p["wc"], p["wo"])
    x2d = _peer(x2d, p)
    kv_shape = (n, WINDOW, A_KV_HEADS, A_HEAD_DIM)
    k_buf = kn[:, t - WINDOW:, :].reshape(kv_shape)
    v_buf = z_a.reshape(n, t, -1)[:, t - WINDOW:, A_Q_COLS + A_KV_COLS:].reshape(kv_shape)
    pool_buf = z_pool.reshape(n, t, POOL_W)[:, t - POOL_BUF:, :]
    return x2d.reshape(n, t, D_MODEL), (k_buf, v_buf, pool_buf, c_new, n_new, m_pad[:, 0:C_HEADS, 0])


def _layer_sample(x, st, p, c):
    n = x.shape[0]
    x2d = x.reshape(n, D_MODEL)
    z_a, z_pool, z_c, z_if, z_g = _proj(x2d, p["norm1_g"], p["w_all"])
    ya, k_buf, v_buf = _swa_sample(z_a, st["k"], st["v"], c["bias_s"], p["sink"], p["qg64"], p["kg"],
                                   c["bd"][0:A_KV_COLS, 0:A_KV_COLS])
    yb = _pool_sample(z_pool, st["pool"], p["w_pool"], p["pool_scale"])
    hc, c_new, n_new, m_new = _mlstm_sample(z_c[:, 0:3 * C_W], z_if, p["gate_bias"], st["c"], st["n"], st["m"])
    x2d = _merge(x2d, ya, yb, hc, z_c, z_g, p["hg"], p["wa"], p["wb"], p["wc"], p["wo"])
    x2d = _peer(x2d, p)
    pool_buf = jnp.concatenate([st["pool"][:, 1:, :], z_pool[:, None, :]], axis=1)
    return x2d.reshape(n, 1, D_MODEL), (k_buf, v_buf, pool_buf, c_new, n_new, m_new)


def kernel(x_prompt, x_sample, cache_k, cache_v, state_pool, state_mlstm_c, state_mlstm_n, state_mlstm_m, rel_bias, norm1_g, w_in, q_norm_g, k_norm_g, attn_sink, w_pool, pool_scale, b_igate, b_fgate, h_norm_g, w_br_a, w_br_b, w_br_c, w_out, norm2_g, peer_wq, peer_k1, peer_k2, peer_u, peer_v):
    a = dict(norm1_g=norm1_g, w_in=w_in, q_norm_g=q_norm_g, k_norm_g=k_norm_g, attn_sink=attn_sink, w_pool=w_pool,
             pool_scale=pool_scale, b_igate=b_igate, b_fgate=b_fgate, h_norm_g=h_norm_g, w_br_a=w_br_a,
             w_br_b=w_br_b, w_br_c=w_br_c, w_out=w_out, norm2_g=norm2_g, peer_wq=peer_wq, peer_k1=peer_k1,
             peer_k2=peer_k2, peer_u=peer_u, peer_v=peer_v)
    depth = w_in.shape[0]
    bd, tri, sel = _constants()
    bias_p, bias_s = _bias_tables(rel_bias)
    consts = dict(bd=bd, tri=tri, bias_p=bias_p, bias_s=bias_s)
    xp, xs = x_prompt, x_sample
    new_p = [[] for _ in range(6)]
    new_s = [[] for _ in range(6)]
    for l in range(depth):
        p = _layer_params(l, a)
        p["sel"] = sel
        p["selt"] = sel.T
        st_s = dict(k=cache_k[l], v=cache_v[l], pool=state_pool[l], c=state_mlstm_c[l], n=state_mlstm_n[l],
                    m=state_mlstm_m[l])
        xp, sp = _layer_prompt(xp, p, consts)
        xs, ss = _layer_sample(xs, st_s, p, consts)
        for i in range(6):
            new_p[i].append(sp[i])
            new_s[i].append(ss[i])
    outs_p = [jnp.stack(v) for v in new_p]
    outs_s = [jnp.stack(v) for v in new_s]
    return (xp, xs, *outs_p, *outs_s)
```

```python
import functools
import math

import jax
import jax.numpy as jnp
import numpy as np
from jax import lax
from jax.experimental import pallas as pl
from jax.experimental.pallas import tpu as pltpu

f32 = jnp.float32
bf16 = jnp.bfloat16
i32 = jnp.int32

D_MODEL = 1024
A_HEADS = 8
A_KV_HEADS = 2
A_HEAD_DIM = 64
A_GROUP = A_HEADS // A_KV_HEADS
WINDOW = 128
N_BUCKETS = 32
MAX_DISTANCE = 128
A_Q_COLS = A_HEADS * A_HEAD_DIM
A_KV_COLS = A_KV_HEADS * A_HEAD_DIM
POOL_WINDOWS = (2, 4, 8, 16)
POOL_GROUP_W = 128
POOL_W = 512
POOL_BUF = 15
C_HEADS = 4
C_HEAD_DIM = 128
C_W = C_HEADS * C_HEAD_DIM
C_CHUNK = 128
PAST_LEN = 16384
PEER_HEADS = 8
PEER_DK = 256
N_KEYS = 128
N_EXPERTS = N_KEYS * N_KEYS
PEER_TOPK = 16
PEER_PICKS = PEER_HEADS * PEER_TOPK
EPS = 1e-6
NEG_INF = float("-inf")

LANES = 128
ROW_CHUNKS = D_MODEL // LANES
HALF_CHUNKS = ROW_CHUNKS // 2
MIB = 1024 * 1024
PEER_TB = 128
TOK_UNROLL = 8
GROUP_TOK = 2 * TOK_UNROLL
GROUP_IDX = GROUP_TOK * PEER_PICKS

PROJ_SPLITS = (A_Q_COLS + 2 * A_KV_COLS, POOL_W, 4 * C_W, LANES, 3 * D_MODEL)

STAIR_ROWS = tuple(PEER_TOPK // (a + 1) for a in range(PEER_TOPK))
STAIR_N = sum(STAIR_ROWS)
STAIR_PAD = (-STAIR_N) % 8


def _cparams(sem, vmem_mib):
    return pltpu.CompilerParams(dimension_semantics=sem, vmem_limit_bytes=vmem_mib * MIB)


def _resident(shape):
    nd = len(shape)
    return pl.BlockSpec(shape, lambda *_: (0,) * nd, pipeline_mode=pl.Buffered(1))


def _dot(a, b):
    return jnp.dot(a, b, preferred_element_type=f32)


def _dot_nt(a, b):
    return lax.dot_general(a, b, (((1,), (1,)), ((), ())), preferred_element_type=f32)


def _dot_tn(a, b):
    return lax.dot_general(a, b, (((0,), (0,)), ((), ())), preferred_element_type=f32)


def _split_bf16(x):
    hi = x.astype(bf16)
    lo = (x - hi.astype(f32)).astype(bf16)
    return hi, lo


def _sigmoid(x):
    return 1.0 / (1.0 + jnp.exp(-x))


def _log_sigmoid(x):
    return jnp.minimum(x, 0.0) - jnp.log(1.0 + jnp.exp(-jnp.abs(x)))


def _rms_rows(x, g):
    ms = jnp.mean(x * x, axis=-1, keepdims=True)
    return x * lax.rsqrt(ms + EPS) * g


def _headnorm(x, blockdiag, g, width):
    hi, lo = _split_bf16(x * x)
    ss = _dot(hi, blockdiag) + _dot(lo, blockdiag)
    return x * lax.rsqrt(ss * (1.0 / width) + EPS) * g


def _proj_kernel(x_ref, g_ref, w_ref, *o_refs):
    xn = _rms_rows(x_ref[...], g_ref[...]).astype(bf16)
    off = 0
    for o_ref, width in zip(o_refs, PROJ_SPLITS):
        o_ref[...] = _dot(xn, w_ref[:, off:off + width])
        off += width


def _proj(x2d, g, w_all):
    m = x2d.shape[0]
    tm = min(256, m)
    total = sum(PROJ_SPLITS)
    return pl.pallas_call(
        _proj_kernel,
        grid=(m // tm,),
        in_specs=[pl.BlockSpec((tm, D_MODEL), lambda i: (i, 0)),
                  _resident((1, D_MODEL)),
                  _resident((D_MODEL, total))],
        out_specs=[pl.BlockSpec((tm, w), lambda i: (i, 0)) for w in PROJ_SPLITS],
        out_shape=[jax.ShapeDtypeStruct((m, w), f32) for w in PROJ_SPLITS],
        compiler_params=_cparams(("parallel",), 48),
        name="proj",
    )(x2d, g.reshape(1, D_MODEL), w_all)


def _swa_kernel(sink_ref, zc_ref, kp_ref, vp_ref, bias_ref, qg_ref, kg_ref, bd_ref, ya_ref, kn_ref):
    i = pl.program_id(1)
    zc = zc_ref[0]
    bd = bd_ref[...]
    bdk = bd[0:A_KV_COLS, 0:A_KV_COLS]
    qn = _headnorm(zc[:, 0:A_Q_COLS], bd, qg_ref[...], A_HEAD_DIM)
    kcn = _headnorm(zc[:, A_Q_COLS:A_Q_COLS + A_KV_COLS], bdk, kg_ref[...], A_HEAD_DIM)
    kpn = _headnorm(kp_ref[0], bdk, kg_ref[...], A_HEAD_DIM)
    kn_ref[0] = kcn
    vc = zc[:, A_Q_COLS + A_KV_COLS:A_Q_COLS + 2 * A_KV_COLS]
    kcat = jnp.concatenate([kpn, kcn], axis=0).astype(bf16)
    vcat = jnp.concatenate([vp_ref[0], vc], axis=0).astype(bf16)
    qb = qn.astype(bf16)
    col = lax.broadcasted_iota(i32, (WINDOW, 2 * WINDOW), 1)
    has_prev = jnp.logical_or(col >= WINDOW, i > 0)
    outs = []
    for h in range(A_HEADS):
        kv = h // A_GROUP
        ks = slice(kv * A_HEAD_DIM, (kv + 1) * A_HEAD_DIM)
        logits = _dot_nt(qb[:, h * A_HEAD_DIM:(h + 1) * A_HEAD_DIM], kcat[:, ks]) * (A_HEAD_DIM ** -0.5)
        logits = jnp.where(has_prev, logits + bias_ref[h], NEG_INF)
        sink = sink_ref[h]
        mx = jnp.maximum(jnp.max(logits, axis=-1, keepdims=True), sink)
        p = jnp.exp(logits - mx)
        p = p / (jnp.sum(p, axis=-1, keepdims=True) + jnp.exp(sink - mx))
        outs.append(_dot(p.astype(bf16), vcat[:, ks]))
    ya_ref[0] = jnp.concatenate(outs, axis=-1)


def _swa_prompt(z_a, bias_tab, sink, qg, kg, bd, n, t):
    nb = t // WINDOW
    z3 = z_a.reshape(n, t, PROJ_SPLITS[0])
    kblk = A_Q_COLS // A_KV_COLS
    prev = lambda b, i: (b, jnp.maximum(i - 1, 0), kblk)
    prev_v = lambda b, i: (b, jnp.maximum(i - 1, 0), kblk + 1)
    ya, kn = pl.pallas_call(
        _swa_kernel,
        grid=(n, nb),
        in_specs=[pl.BlockSpec(memory_space=pltpu.SMEM),
                  pl.BlockSpec((1, WINDOW, PROJ_SPLITS[0]), lambda b, i: (b, i, 0)),
                  pl.BlockSpec((1, WINDOW, A_KV_COLS), prev),
                  pl.BlockSpec((1, WINDOW, A_KV_COLS), prev_v),
                  _resident((A_HEADS, WINDOW, 2 * WINDOW)),
                  _resident((1, A_Q_COLS)),
                  _resident((1, A_KV_COLS)),
                  _resident((A_Q_COLS, A_Q_COLS))],
        out_specs=[pl.BlockSpec((1, WINDOW, A_Q_COLS), lambda b, i: (b, i, 0)),
                   pl.BlockSpec((1, WINDOW, A_KV_COLS), lambda b, i: (b, i, 0))],
        out_shape=[jax.ShapeDtypeStruct((n, t, A_Q_COLS), f32),
                   jax.ShapeDtypeStruct((n, t, A_KV_COLS), f32)],
        compiler_params=_cparams(("parallel", "arbitrary"), 32),
        name="swa_prompt",
    )(sink, z3, z3, z3, bias_tab, qg, kg, bd)
    return ya, kn


def _swa_sample_kernel(q_ref, kn_ref, vn_ref, ck_ref, cv_ref, bias_ref, sink_ref, qg_ref, kg_ref, bd_ref,
                       ya_ref, ko_ref, vo_ref):
    q = q_ref[...]
    qn = q * lax.rsqrt(jnp.mean(q * q, axis=-1, keepdims=True) + EPS) * qg_ref[...]
    kn = _headnorm(kn_ref[...], bd_ref[...], kg_ref[...], A_HEAD_DIM)
    rowi = lax.broadcasted_iota(i32, ck_ref.shape, 1)
    newest = rowi == WINDOW - 1
    k_all = jnp.where(newest, kn[:, None, :], pltpu.roll(ck_ref[...], WINDOW - 1, 1))
    v_all = jnp.where(newest, vn_ref[...][:, None, :], pltpu.roll(cv_ref[...], WINDOW - 1, 1))
    ko_ref[...] = k_all
    vo_ref[...] = v_all
    outs = []
    for kv in range(A_KV_HEADS):
        ks = slice(kv * A_HEAD_DIM, (kv + 1) * A_HEAD_DIM)
        hs = slice(kv * A_GROUP, (kv + 1) * A_GROUP)
        kh = k_all[:, :, ks].astype(bf16)
        vh = v_all[:, :, ks].astype(bf16)
        qh = qn[:, hs, :].astype(bf16)
        logits = jnp.einsum('bgd,bsd->bgs', qh, kh, preferred_element_type=f32) * (A_HEAD_DIM ** -0.5)
        logits = logits + bias_ref[hs, :][None]
        sink = sink_ref[hs, :][None]
        mx = jnp.maximum(jnp.max(logits, axis=-1, keepdims=True), sink)
        p = jnp.exp(logits - mx)
        p = p / (jnp.sum(p, axis=-1, keepdims=True) + jnp.exp(sink - mx))
        outs.append(jnp.einsum('bgs,bsd->bgd', p.astype(bf16), vh, preferred_element_type=f32))
    ya_ref[...] = jnp.concatenate(outs, axis=1)


def _swa_sample(z_a, cache_k, cache_v, bias_s, sink, qg64, kg, bdk):
    n = z_a.shape[0]
    bn = 8
    q3 = z_a[:, 0:A_Q_COLS].reshape(n, A_HEADS, A_HEAD_DIM)
    k_new = z_a[:, A_Q_COLS:A_Q_COLS + A_KV_COLS]
    v_new = z_a[:, A_Q_COLS + A_KV_COLS:]
    ck = cache_k.reshape(n, WINDOW, A_KV_COLS)
    cv = cache_v.reshape(n, WINDOW, A_KV_COLS)
    row2 = pl.BlockSpec((bn, A_KV_COLS), lambda i: (i, 0))
    cache = pl.BlockSpec((bn, WINDOW, A_KV_COLS), lambda i: (i, 0, 0))
    q_spec = pl.BlockSpec((bn, A_HEADS, A_HEAD_DIM), lambda i: (i, 0, 0))
    ya, ko, vo = pl.pallas_call(
        _swa_sample_kernel,
        grid=(n // bn,),
        in_specs=[q_spec, row2, row2, cache, cache,
                  _resident((A_HEADS, WINDOW)), _resident((A_HEADS, 1)),
                  _resident((1, A_HEAD_DIM)), _resident((1, A_KV_COLS)), _resident((A_KV_COLS, A_KV_COLS))],
        out_specs=[q_spec, cache, cache],
        out_shape=[jax.ShapeDtypeStruct((n, A_HEADS, A_HEAD_DIM), f32),
                   jax.ShapeDtypeStruct((n, WINDOW, A_KV_COLS), f32),
                   jax.ShapeDtypeStruct((n, WINDOW, A_KV_COLS), f32)],
        compiler_params=_cparams(("parallel",), 32),
        name="swa_sample",
    )(q3, k_new, v_new, ck, cv, bias_s, sink.reshape(A_HEADS, 1), qg64, kg, bdk)
    shape = (n, WINDOW, A_KV_HEADS, A_HEAD_DIM)
    return ya.reshape(n, A_Q_COLS), ko.reshape(shape), vo.reshape(shape)


def _pool_kernel(x_ref, w_ref, scale_ref, y_ref, buf_ref):
    j = pl.program_id(1)
    bt = x_ref.shape[1]
    halo = POOL_BUF + 1

    @pl.when(j == 0)
    def _():
        buf_ref[0:halo, :] = jnp.zeros((halo, POOL_W), f32)

    x = x_ref[0]
    buf_ref[halo:halo + bt, :] = x
    pos = j * bt + lax.broadcasted_iota(i32, (bt, 1), 0)
    outs = []
    for g, w in enumerate(POOL_WINDOWS):
        cs = slice(g * POOL_GROUP_W, (g + 1) * POOL_GROUP_W)
        xg = x[:, cs]
        s = xg
        for k in range(1, w):
            s = s + buf_ref[halo - k:halo - k + bt, cs]
        cnt = jnp.minimum(pos + 1, w).astype(f32)
        pooled = s / cnt - xg
        outs.append(_dot(pooled.astype(bf16), w_ref[g]))
    y_ref[0] = jnp.concatenate(outs, axis=-1) * scale_ref[...]
    buf_ref[0:halo, :] = x[bt - halo:bt, :]


def _pool_prompt(z_pool, w_pool, scale, n, t):
    bt = min(512, t)
    z3 = z_pool.reshape(n, t, POOL_W)
    spec = pl.BlockSpec((1, bt, POOL_W), lambda b, j: (b, j, 0))
    return pl.pallas_call(
        _pool_kernel,
        grid=(n, t // bt),
        in_specs=[spec, _resident((4, POOL_GROUP_W, POOL_GROUP_W)), _resident((1, POOL_W))],
        out_specs=spec,
        out_shape=jax.ShapeDtypeStruct((n, t, POOL_W), f32),
        scratch_shapes=[pltpu.VMEM((bt + POOL_BUF + 1, POOL_W), f32)],
        compiler_params=_cparams(("parallel", "arbitrary"), 32),
        name="pool_prompt",
    )(z3, w_pool, scale)


def _pool_sample_kernel(x_ref, st_ref, w_ref, scale_ref, y_ref):
    x = x_ref[...]
    outs = []
    for g, w in enumerate(POOL_WINDOWS):
        cs = slice(g * POOL_GROUP_W, (g + 1) * POOL_GROUP_W)
        xg = x[:, cs]
        s = xg
        for k in range(1, w):
            s = s + st_ref[:, POOL_BUF - k, cs]
        cnt = float(min(PAST_LEN + 1, w))
        pooled = s / cnt - xg
        outs.append(_dot(pooled.astype(bf16), w_ref[g]))
    y_ref[...] = jnp.concatenate(outs, axis=-1) * scale_ref[...]


def _pool_sample(z_pool, state, w_pool, scale):
    n = z_pool.shape[0]
    bn = 8
    return pl.pallas_call(
        _pool_sample_kernel,
        grid=(n // bn,),
        in_specs=[pl.BlockSpec((bn, POOL_W), lambda i: (i, 0)),
                  pl.BlockSpec((bn, POOL_BUF, POOL_W), lambda i: (i, 0, 0)),
                  _resident((4, POOL_GROUP_W, POOL_GROUP_W)), _resident((1, POOL_W))],
        out_specs=pl.BlockSpec((bn, POOL_W), lambda i: (i, 0)),
        out_shape=jax.ShapeDtypeStruct((n, POOL_W), f32),
        compiler_params=_cparams(("parallel",), 32),
        name="pool_sample",
    )(z_pool, state, w_pool, scale)


def _mlstm_kernel(zc_ref, zif_ref, gb_ref, tri_ref, h_ref, c_out, n_out, m_out, c_s, n_s, m_s):
    j = pl.program_id(1)
    L = C_CHUNK

    @pl.when(j == 0)
    def _():
        c_s[...] = jnp.zeros(c_s.shape, f32)
        n_s[...] = jnp.zeros(n_s.shape, f32)
        m_s[...] = jnp.zeros(m_s.shape, f32)

    zc = zc_ref[0]
    gates = zif_ref[0] + gb_ref[...]
    lf_hi, lf_lo = _split_bf16(_log_sigmoid(gates))
    bcum = _dot(tri_ref[...], lf_hi) + _dot(tri_ref[...], lf_lo)
    bcum_t = bcum.T
    gates_t = gates.T
    row = lax.broadcasted_iota(i32, (L, L), 0)
    col = lax.broadcasted_iota(i32, (L, L), 1)
    causal = col <= row
    for h in range(C_HEADS):
        hs = slice(h * C_HEAD_DIM, (h + 1) * C_HEAD_DIM)
        q = zc[:, hs]
        k = zc[:, C_W + h * C_HEAD_DIM:C_W + (h + 1) * C_HEAD_DIM] * (C_HEAD_DIM ** -0.5)
        v = zc[:, 2 * C_W + h * C_HEAD_DIM:2 * C_W + (h + 1) * C_HEAD_DIM]
        b_col = bcum[:, C_HEADS + h:C_HEADS + h + 1]
        b_row = bcum_t[C_HEADS + h:C_HEADS + h + 1, :]
        ig_col = gates[:, h:h + 1]
        ig_row = gates_t[h:h + 1, :]
        m_prev = m_s[h:h + 1, 0:1]
        dmat = jnp.where(causal, b_col - b_row + ig_row, NEG_INF)
        inter = b_col + m_prev
        m_t = jnp.maximum(inter, jnp.max(dmat, axis=-1, keepdims=True))
        qb, kb, vb = q.astype(bf16), k.astype(bf16), v.astype(bf16)
        s = _dot_nt(qb, kb) * jnp.exp(dmat - m_t)
        w_inter = jnp.exp(inter - m_t)
        c_prev = c_s[h]
        n_prev = n_s[h:h + 1, :]
        num = _dot(s.astype(bf16), vb) + w_inter * _dot_nt(qb, c_prev.astype(bf16))
        den = jnp.sum(s, axis=-1, keepdims=True) + w_inter * jnp.sum(q * n_prev, axis=-1, keepdims=True)
        h_ref[0, :, hs] = num / jnp.maximum(jnp.abs(den), jnp.exp(-m_t))
        m_new = m_t[L - 1:L, :]
        b_last = b_col[L - 1:L, :]
        w_end = jnp.exp(b_last - b_col + ig_col - m_new)
        decay = jnp.exp(b_last + m_prev - m_new)
        c_s[h] = decay * c_prev + _dot_tn((v * w_end).astype(bf16), kb)
        n_s[h:h + 1, :] = decay * n_prev + jnp.sum(k * w_end, axis=0, keepdims=True)
        m_s[h:h + 1, :] = jnp.broadcast_to(m_new, (1, LANES))

    @pl.when(j == pl.num_programs(1) - 1)
    def _():
        c_out[0] = c_s[...]
        n_out[0] = n_s[0:C_HEADS, :]
        m_out[0] = m_s[...]


def _mlstm_prompt(z_c, z_if, gate_bias, tri, n, t):
    nc = t // C_CHUNK
    zc3 = z_c.reshape(n, t, 4 * C_W)
    zif3 = z_if.reshape(n, t, LANES)
    return pl.pallas_call(
        _mlstm_kernel,
        grid=(n, nc),
        in_specs=[pl.BlockSpec((1, C_CHUNK, 3 * C_W), lambda b, j: (b, j, 0)),
                  pl.BlockSpec((1, C_CHUNK, LANES), lambda b, j: (b, j, 0)),
                  _resident((1, LANES)),
                  _resident((C_CHUNK, C_CHUNK))],
        out_specs=[pl.BlockSpec((1, C_CHUNK, C_W), lambda b, j: (b, j, 0)),
                   pl.BlockSpec((1, C_HEADS, C_HEAD_DIM, C_HEAD_DIM), lambda b, j: (b, 0, 0, 0)),
                   pl.BlockSpec((1, C_HEADS, C_HEAD_DIM), lambda b, j: (b, 0, 0)),
                   pl.BlockSpec((1, 8, LANES), lambda b, j: (b, 0, 0))],
        out_shape=[jax.ShapeDtypeStruct((n, t, C_W), f32),
                   jax.ShapeDtypeStruct((n, C_HEADS, C_HEAD_DIM, C_HEAD_DIM), f32),
                   jax.ShapeDtypeStruct((n, C_HEADS, C_HEAD_DIM), f32),
                   jax.ShapeDtypeStruct((n, 8, LANES), f32)],
        scratch_shapes=[pltpu.VMEM((C_HEADS, C_HEAD_DIM, C_HEAD_DIM), f32),
                        pltpu.VMEM((8, LANES), f32),
                        pltpu.VMEM((8, LANES), f32)],
        compiler_params=_cparams(("parallel", "arbitrary"), 32),
        name="mlstm_prompt",
    )(zc3, zif3, gate_bias, tri)


def _mlstm_sample_kernel(zc_ref, zif_ref, gb_ref, c_ref, n_ref, m_ref, h_ref, co_ref, no_ref, mo_ref):
    bn = zc_ref.shape[0]
    zc = zc_ref[...]
    gates = zif_ref[...] + gb_ref[...]
    lfs = _log_sigmoid(gates)
    lane = lax.broadcasted_iota(i32, (C_HEAD_DIM, LANES), 1)
    pad = jnp.zeros((LANES - bn, C_HEAD_DIM), f32)
    for h in range(C_HEADS):
        hs = slice(h * C_HEAD_DIM, (h + 1) * C_HEAD_DIM)
        q = zc[:, hs]
        k = zc[:, C_W + h * C_HEAD_DIM:C_W + (h + 1) * C_HEAD_DIM] * (C_HEAD_DIM ** -0.5)
        v = zc[:, 2 * C_W + h * C_HEAD_DIM:2 * C_W + (h + 1) * C_HEAD_DIM]
        ig = gates[:, h:h + 1]
        lf = lfs[:, C_HEADS + h:C_HEADS + h + 1]
        m_prev = m_ref[:, h:h + 1]
        n_prev = n_ref[:, h, :]
        inter = lf + m_prev
        m_t = jnp.maximum(inter, ig)
        s = jnp.sum(q * k, axis=-1, keepdims=True) * jnp.exp(ig - m_t)
        w_inter = jnp.exp(inter - m_t)
        w_end = jnp.exp(ig - m_t)
        cq_t = jnp.zeros((C_HEAD_DIM, LANES), f32)
        for b in range(bn):
            colsum = jnp.sum(c_ref[b, h] * q[b:b + 1, :], axis=-1, keepdims=True)
            cq_t = jnp.where(lane == b, colsum, cq_t)
        cq = cq_t.T[0:bn, :]
        num = s * v + w_inter * cq
        den = s + w_inter * jnp.sum(n_prev * q, axis=-1, keepdims=True)
        h_ref[:, hs] = num / jnp.maximum(jnp.abs(den), jnp.exp(-m_t))
        vw_t = jnp.concatenate([v * w_end, pad], axis=0).T
        for b in range(bn):
            co_ref[b, h] = w_inter[b:b + 1, :] * c_ref[b, h] + vw_t[:, b:b + 1] * k[b:b + 1, :]
        no_ref[:, h, :] = w_inter * n_prev + w_end * k
        mo_ref[:, h:h + 1] = m_t


def _mlstm_sample(z_c, z_if, gate_bias, c0, n0, m0):
    n = z_c.shape[0]
    bn = 8
    c_spec = pl.BlockSpec((bn, C_HEADS, C_HEAD_DIM, C_HEAD_DIM), lambda i: (i, 0, 0, 0))
    n_spec = pl.BlockSpec((bn, C_HEADS, C_HEAD_DIM), lambda i: (i, 0, 0))
    m_spec = pl.BlockSpec((bn, C_HEADS), lambda i: (i, 0))
    return pl.pallas_call(
        _mlstm_sample_kernel,
        grid=(n // bn,),
        in_specs=[pl.BlockSpec((bn, 3 * C_W), lambda i: (i, 0)),
                  pl.BlockSpec((bn, LANES), lambda i: (i, 0)),
                  _resident((1, LANES)), c_spec, n_spec, m_spec],
        out_specs=[pl.BlockSpec((bn, C_W), lambda i: (i, 0)), c_spec, n_spec, m_spec],
        out_shape=[jax.ShapeDtypeStruct((n, C_W), f32),
                   jax.ShapeDtypeStruct(c0.shape, f32),
                   jax.ShapeDtypeStruct(n0.shape, f32),
                   jax.ShapeDtypeStruct(m0.shape, f32)],
        compiler_params=_cparams(("parallel",), 32),
        name="mlstm_sample",
    )(z_c, z_if, gate_bias, c0, n0, m0)


def _merge_kernel(x_ref, ya_ref, yb_ref, h_ref, co_ref, zg_ref, hg_ref, wa_ref, wb_ref, wc_ref, wo_ref, o_ref):
    h = h_ref[...]
    parts = []
    for hh in range(C_HEADS):
        hs = h[:, hh * C_HEAD_DIM:(hh + 1) * C_HEAD_DIM]
        parts.append(hs * lax.rsqrt(jnp.mean(hs * hs, axis=-1, keepdims=True) + EPS))
    yc = jnp.concatenate(parts, axis=-1) * hg_ref[...] * _sigmoid(co_ref[...])
    zg = zg_ref[...]
    merged = (_sigmoid(zg[:, 0:D_MODEL]) * _dot(ya_ref[...].astype(bf16), wa_ref[...])
              + _sigmoid(zg[:, D_MODEL:2 * D_MODEL]) * _dot(yb_ref[...].astype(bf16), wb_ref[...])
              + _sigmoid(zg[:, 2 * D_MODEL:]) * _dot(yc.astype(bf16), wc_ref[...]))
    o_ref[...] = x_ref[...] + _dot(merged.astype(bf16), wo_ref[...])


def _merge(x2d, ya, yb, hc, z_c, z_g, hg, wa, wb, wc, wo):
    m = x2d.shape[0]
    tb = min(256, m)
    row = lambda w: pl.BlockSpec((tb, w), lambda i: (i, 0))
    return pl.pallas_call(
        _merge_kernel,
        grid=(m // tb,),
        in_specs=[row(D_MODEL), row(A_Q_COLS), row(POOL_W), row(C_W),
                  pl.BlockSpec((tb, C_W), lambda i: (i, 3)),
                  row(3 * D_MODEL), _resident((1, C_W)),
                  _resident((A_Q_COLS, D_MODEL)), _resident((POOL_W, D_MODEL)), _resident((C_W, D_MODEL)),
                  _resident((D_MODEL, D_MODEL))],
        out_specs=row(D_MODEL),
        out_shape=jax.ShapeDtypeStruct((m, D_MODEL), f32),
        compiler_params=_cparams(("parallel",), 48),
        name="merge",
    )(x2d, ya, yb, hc, z_c, z_g, hg, wa, wb, wc, wo)


def _topk_rows(s, k):
    nrows = s.shape[0]
    riota = lax.broadcasted_iota(i32, s.shape, 0).astype(f32)
    vals, idxs = [], []
    for _ in range(k):
        m = jnp.max(s, axis=0, keepdims=True)
        i = jnp.min(jnp.where(s == m, riota, float(nrows)), axis=0, keepdims=True)
        vals.append(m)
        idxs.append(i)
        s = jnp.where(riota == i, NEG_INF, s)
    return vals, idxs


def _pack_pairs(x):
    bits = pltpu.bitcast(x, i32)
    r = bits + (jnp.int32(0x7FFF) + (lax.shift_right_logical(bits, 16) & 1))
    return [lax.shift_right_logical(r[:, 2 * s * LANES:(2 * s + 1) * LANES], 16)
            | (r[:, (2 * s + 1) * LANES:(2 * s + 2) * LANES] & jnp.int32(-65536))
            for s in range(HALF_CHUNKS)]


def _route_kernel(x_ref, g_ref, wq_ref, k1_ref, k2_ref, idx_ref, gate_ref, xp_ref):
    tb = x_ref.shape[0]
    xn = _rms_rows(x_ref[...], g_ref[...])
    for s, words in enumerate(_pack_pairs(xn)):
        xp_ref[:, s, :] = words
    qb = _dot(xn.astype(bf16), wq_ref[...]).astype(bf16)
    half = PEER_DK // 2
    exp_rows, gate_rows = [], []
    for h in range(PEER_HEADS):
        s1 = _dot_nt(k1_ref[...], qb[:, h * PEER_DK:h * PEER_DK + half])
        s2 = _dot_nt(k2_ref[...], qb[:, h * PEER_DK + half:(h + 1) * PEER_DK])
        v1, i1 = _topk_rows(s1, PEER_TOPK)
        v2, i2 = _topk_rows(s2, PEER_TOPK)
        v2s = jnp.concatenate(v2, axis=0)
        i2s = jnp.concatenate(i2, axis=0)
        cv, ce = [], []
        for a, nb in enumerate(STAIR_ROWS):
            cv.append(v1[a] + v2s[0:nb])
            ce.append((i1[a] * float(N_KEYS) + i2s[0:nb]) * float(HALF_CHUNKS))
        if STAIR_PAD:
            cv.append(jnp.full((STAIR_PAD, tb), NEG_INF, f32))
            ce.append(jnp.zeros((STAIR_PAD, tb), f32))
        cand = jnp.concatenate(cv, axis=0)
        cexp = jnp.concatenate(ce, axis=0)
        riota = lax.broadcasted_iota(i32, cand.shape, 0).astype(f32)
        tv, ti = _topk_rows(cand, PEER_TOPK)
        te = [jnp.max(jnp.where(riota == i, cexp, -1.0), axis=0, keepdims=True) for i in ti]
        tvs = jnp.concatenate(tv, axis=0)
        ex = jnp.exp(tvs - tv[0])
        gate_rows.append(ex / jnp.sum(ex, axis=0, keepdims=True))
        exp_rows.append(jnp.concatenate(te, axis=0))
    idx_ref[...] = jnp.concatenate(exp_rows, axis=0).T.astype(i32)
    gate_ref[...] = jnp.concatenate(gate_rows, axis=0).T


def _route(x2d, g, wq, k1, k2):
    m = x2d.shape[0]
    tb = min(256, m)
    return pl.pallas_call(
        _route_kernel,
        grid=(m // tb,),
        in_specs=[pl.BlockSpec((tb, D_MODEL), lambda i: (i, 0)),
                  _resident((1, D_MODEL)),
                  _resident((D_MODEL, PEER_HEADS * PEER_DK)),
                  _resident((N_KEYS, PEER_DK // 2)), _resident((N_KEYS, PEER_DK // 2))],
        out_specs=[pl.BlockSpec((tb, PEER_PICKS), lambda i: (i, 0)),
                   pl.BlockSpec((tb, PEER_PICKS), lambda i: (i, 0)),
                   pl.BlockSpec((tb, HALF_CHUNKS, LANES), lambda i: (i, 0, 0))],
        out_shape=[jax.ShapeDtypeStruct((m, PEER_PICKS), i32),
                   jax.ShapeDtypeStruct((m, PEER_PICKS), f32),
                   jax.ShapeDtypeStruct((m, HALF_CHUNKS, LANES), i32)],
        compiler_params=_cparams(("parallel",), 48),
        name="peer_route",
    )(x2d, g.reshape(1, D_MODEL), wq, k1, k2)


def _pack_table_kernel(w_ref, o_ref):
    rows = w_ref.shape[0]
    for s, words in enumerate(_pack_pairs(w_ref[...])):
        o_ref[pl.ds(s, rows, stride=HALF_CHUNKS), :] = words


def _pack_table(w):
    e = w.shape[0]
    rows = 512
    return pl.pallas_call(
        _pack_table_kernel,
        grid=(e // rows,),
        in_specs=[pl.BlockSpec((rows, D_MODEL), lambda i: (i, 0))],
        out_specs=pl.BlockSpec((rows * HALF_CHUNKS, LANES), lambda i: (i, 0)),
        out_shape=jax.ShapeDtypeStruct((e * HALF_CHUNKS, LANES), i32),
        compiler_params=_cparams(("parallel",), 32),
        name="pack_table",
    )(w)


def _gather_groups(idx_hbm, idx_bufs, sem, tb, group_fn):
    assert tb % (2 * GROUP_TOK) == 0, tb
    ngroups = tb // GROUP_TOK
    npairs = ngroups // 2
    first = pl.program_id(0) * ngroups

    def copy(grp, slot):
        src = idx_hbm.at[pl.ds((first + grp) * GROUP_IDX, GROUP_IDX)]
        return pltpu.make_async_copy(src, idx_bufs[slot], sem.at[slot])

    copy(0, 0).start()

    def pair(i2, carry):
        copy(2 * i2 + 1, 1).start()
        copy(2 * i2, 0).wait()
        group_fn(pl.multiple_of(2 * i2 * GROUP_TOK, GROUP_TOK), idx_bufs[0])

        @pl.when(i2 + 1 < npairs)
        def _():
            copy(2 * i2 + 2, 0).start()

        copy(2 * i2 + 1, 1).wait()
        group_fn(pl.multiple_of((2 * i2 + 1) * GROUP_TOK, GROUP_TOK), idx_bufs[1])
        return carry

    lax.fori_loop(0, npairs, pair, 0)


def _gather_rows(tab_ref, idx_smem, c):
    rows = []
    for j in range(PEER_PICKS):
        off = pl.multiple_of(idx_smem[c * PEER_PICKS + j], HALF_CHUNKS)
        rows.append(tab_ref[pl.ds(off, HALF_CHUNKS), :])
    return pltpu.bitcast(jnp.concatenate(rows, axis=0), bf16)


def _chunk_diag():
    lane = lax.broadcasted_iota(i32, (ROW_CHUNKS, ROW_CHUNKS * PEER_PICKS), 1)
    sub = lax.broadcasted_iota(i32, (ROW_CHUNKS, ROW_CHUNKS * PEER_PICKS), 0)
    return (lane & (ROW_CHUNKS - 1)) == sub, sub


def _peer_u_kernel(idx_hbm, x_ref, gate_ref, selt_ref, tab_ref, w_ref, s0_ref, s1_ref, sem):
    diag, sub = _chunk_diag()

    def group(tg, idx_smem):
        for tile in range(GROUP_TOK // TOK_UNROLL):
            t0 = pl.multiple_of(tg + tile * TOK_UNROLL, TOK_UNROLL)
            acc = jnp.zeros((TOK_UNROLL, ROW_CHUNKS * PEER_PICKS), f32)
            for c in range(TOK_UNROLL):
                picked = _gather_rows(tab_ref, idx_smem, tile * TOK_UNROLL + c)
                xb = pltpu.bitcast(x_ref[t0 + c], bf16)
                out = _dot_nt(xb, picked)
                rowsum = jnp.sum(jnp.where(diag, out, 0.0), axis=0, keepdims=True)
                acc = jnp.where(sub == c, rowsum, acc)
            hi, lo = _split_bf16(acc)
            a = _dot(hi, selt_ref[...]) + _dot(lo, selt_ref[...])
            gelu = 0.5 * a * (1.0 + lax.erf(a * (2.0 ** -0.5)))
            w_ref[pl.ds(t0, TOK_UNROLL), :] = gate_ref[pl.ds(t0, TOK_UNROLL), :] * gelu

    _gather_groups(idx_hbm, (s0_ref, s1_ref), sem, x_ref.shape[0], group)


def _peer_scratch():
    return [pltpu.SMEM((GROUP_IDX,), i32),
            pltpu.SMEM((GROUP_IDX,), i32),
            pltpu.SemaphoreType.DMA((2,))]


def _peer_u(idx_flat, xp, gates, selt, tab):
    m = xp.shape[0]
    tb = min(PEER_TB, m)
    return pl.pallas_call(
        _peer_u_kernel,
        grid=(m // tb,),
        in_specs=[pl.BlockSpec(memory_space=pl.ANY),
                  pl.BlockSpec((tb, HALF_CHUNKS, LANES), lambda i: (i, 0, 0)),
                  pl.BlockSpec((tb, PEER_PICKS), lambda i: (i, 0)),
                  _resident((ROW_CHUNKS * PEER_PICKS, PEER_PICKS)),
                  _resident((N_EXPERTS * HALF_CHUNKS, LANES))],
        out_specs=pl.BlockSpec((tb, PEER_PICKS), lambda i: (i, 0)),
        out_shape=jax.ShapeDtypeStruct((m, PEER_PICKS), f32),
        scratch_shapes=_peer_scratch(),
        compiler_params=_cparams(("arbitrary",), 44),
        name="peer_u",
    )(idx_flat, xp, gates, selt, tab)


def _peer_v_kernel(idx_hbm, w_ref, x_ref, sel_ref, tab_ref, o_ref, s0_ref, s1_ref, sem):
    diag, _ = _chunk_diag()

    def group(tg, idx_smem):
        for tile in range(GROUP_TOK // TOK_UNROLL):
            t0 = pl.multiple_of(tg + tile * TOK_UNROLL, TOK_UNROLL)
            w8 = _dot(w_ref[pl.ds(t0, TOK_UNROLL), :].astype(bf16), sel_ref[...])
            rows = [jnp.zeros((TOK_UNROLL, LANES), f32) for _ in range(ROW_CHUNKS)]
            for c in range(TOK_UNROLL):
                picked = _gather_rows(tab_ref, idx_smem, tile * TOK_UNROLL + c)
                wsel = jnp.where(diag, w8[c:c + 1, :], 0.0).astype(bf16)
                y = _dot(wsel, picked)
                rows = [jnp.where(tok == c, y[r:r + 1, :], rows[r]) for r in range(ROW_CHUNKS)]
            o_ref[pl.ds(t0, TOK_UNROLL), :] = x_ref[pl.ds(t0, TOK_UNROLL), :] + jnp.concatenate(rows, axis=-1)

    tok = lax.broadcasted_iota(i32, (TOK_UNROLL, LANES), 0)
    _gather_groups(idx_hbm, (s0_ref, s1_ref), sem, x_ref.shape[0], group)


def _peer_v(idx_flat, w2d, x2d, sel, tab):
    m = x2d.shape[0]
    tb = min(PEER_TB, m)
    return pl.pallas_call(
        _peer_v_kernel,
        grid=(m // tb,),
        in_specs=[pl.BlockSpec(memory_space=pl.ANY),
                  pl.BlockSpec((tb, PEER_PICKS), lambda i: (i, 0)),
                  pl.BlockSpec((tb, D_MODEL), lambda i: (i, 0)),
                  _resident((PEER_PICKS, ROW_CHUNKS * PEER_PICKS)),
                  _resident((N_EXPERTS * HALF_CHUNKS, LANES))],
        out_specs=pl.BlockSpec((tb, D_MODEL), lambda i: (i, 0)),
        out_shape=jax.ShapeDtypeStruct((m, D_MODEL), f32),
        scratch_shapes=_peer_scratch(),
        compiler_params=_cparams(("arbitrary",), 44),
        name="peer_v",
    )(idx_flat, w2d, x2d, sel, tab)


def _peer(x2d, p):
    m = x2d.shape[0]
    idx, gates, xp = _route(x2d, p["norm2_g"], p["wq"], p["k1"], p["k2"])
    idx_flat = idx.reshape(m * PEER_PICKS)
    w2d = _peer_u(idx_flat, xp, gates, p["selt"], p["u_tab"])
    return _peer_v(idx_flat, w2d, x2d, p["sel"], p["v_tab"])


def _rel_bucket(dist):
    max_exact = N_BUCKETS // 2
    d = jnp.maximum(dist, 0)
    df = jnp.maximum(d, 1).astype(f32)
    large = max_exact + (jnp.log(df / max_exact) / math.log(MAX_DISTANCE / max_exact)
                         * (N_BUCKETS - max_exact)).astype(i32)
    large = jnp.minimum(large, N_BUCKETS - 1)
    return jnp.where(d < max_exact, d, large)


def _bias_tables(rel_bias):
    by_dist = rel_bias[_rel_bucket(jnp.arange(2 * WINDOW))].astype(f32).T
    dist = np.arange(WINDOW)[:, None] + WINDOW - np.arange(2 * WINDOW)[None, :]
    valid = (dist >= 0) & (dist < WINDOW)
    padded = jnp.pad(by_dist, ((0, 0), (WINDOW - 1, 0)))
    rows = [padded[:, q:q + 2 * WINDOW][:, ::-1] for q in range(WINDOW)]
    bias_p = jnp.where(jnp.asarray(valid)[None], jnp.stack(rows, axis=1), NEG_INF)
    bias_s = by_dist[:, 0:WINDOW][:, ::-1]
    return bias_p, bias_s


def _layer_params(l, a):
    w = a["w_in"][l]
    o_pin = A_Q_COLS + 2 * A_KV_COLS
    o_c = o_pin + POOL_W
    o_if = o_c + 3 * C_W
    o_co = o_if + 2 * C_HEADS
    o_g = o_co + C_W
    w_if = jnp.pad(w[:, o_if:o_co], ((0, 0), (0, LANES - 2 * C_HEADS)))
    w_all = jnp.concatenate([w[:, 0:o_pin], w[:, o_pin:o_c], w[:, o_c:o_if], w[:, o_co:o_g], w_if, w[:, o_g:]],
                            axis=1).astype(bf16)
    gate_bias = jnp.pad(jnp.concatenate([a["b_igate"][l], a["b_fgate"][l]]), (0, LANES - 2 * C_HEADS))
    return dict(
        norm1_g=a["norm1_g"][l], w_all=w_all,
        qg=jnp.tile(a["q_norm_g"][l], A_HEADS).reshape(1, A_Q_COLS),
        qg64=a["q_norm_g"][l].reshape(1, A_HEAD_DIM),
        kg=jnp.tile(a["k_norm_g"][l], A_KV_HEADS).reshape(1, A_KV_COLS),
        sink=a["attn_sink"][l].astype(f32),
        w_pool=a["w_pool"][l].astype(bf16), pool_scale=a["pool_scale"][l].reshape(1, POOL_W),
        gate_bias=gate_bias.reshape(1, LANES).astype(f32),
        hg=jnp.tile(a["h_norm_g"][l], C_HEADS).reshape(1, C_W),
        wa=a["w_br_a"][l].astype(bf16), wb=a["w_br_b"][l].astype(bf16), wc=a["w_br_c"][l].astype(bf16),
        wo=a["w_out"][l].astype(bf16),
        norm2_g=a["norm2_g"][l], wq=a["peer_wq"][l].astype(bf16),
        k1=a["peer_k1"][l].astype(bf16), k2=a["peer_k2"][l].astype(bf16),
        u_tab=_pack_table(a["peer_u"][l]), v_tab=_pack_table(a["peer_v"][l]),
    )


def _constants():
    head = np.arange(A_Q_COLS) // A_HEAD_DIM
    bd = jnp.asarray(head[:, None] == head[None, :], bf16)
    tri = jnp.asarray(np.tril(np.ones((C_CHUNK, C_CHUNK))), bf16)
    sel = jnp.asarray(np.arange(ROW_CHUNKS * PEER_PICKS)[None, :] // ROW_CHUNKS
                      == np.arange(PEER_PICKS)[:, None], bf16)
    return bd, tri, sel


def _layer_prompt(x, p, c):
    n, t, _ = x.shape
    x2d = x.reshape(n * t, D_MODEL)
    z_a, z_pool, z_c, z_if, z_g = _proj(x2d, p["norm1_g"], p["w_all"])
    ya, kn = _swa_prompt(z_a, c["bias_p"], p["sink"], p["qg"], p["kg"], c["bd"], n, t)
    yb = _pool_prompt(z_pool, p["w_pool"], p["pool_scale"], n, t)
    hc, c_new, n_new, m_pad = _mlstm_prompt(z_c, z_if, p["gate_bias"], c["tri"], n, t)
    x2d = _merge(x2d, ya.reshape(n * t, A_Q_COLS), yb.reshape(n * t, POOL_W), hc.reshape(n * t, C_W),
                 z_c, z_g, p["hg"], p["wa"], p["wb"], p["wc"], p["wo"])
    x2d = _peer(x2d, p)
    kv_shape = (n, WINDOW, A_KV_HEADS, A_HEAD_DIM)
    k_buf = kn[:, t - WINDOW:, :].reshape(kv_shape)
    v_buf = z_a.reshape(n, t, -1)[:, t - WINDOW:, A_Q_COLS + A_KV_COLS:].reshape(kv_shape)
    pool_buf = z_pool.reshape(n, t, POOL_W)[:, t - POOL_BUF:, :]
    return x2d.reshape(n, t, D_MODEL), (k_buf, v_buf, pool_buf, c_new, n_new, m_pad[:, 0:C_HEADS, 0])


def _layer_sample(x, st, p, c):
    n = x.shape[0]
    x2d = x.reshape(n, D_MODEL)
    z_a, z_pool, z_c, z_if, z_g = _proj(x2d, p["norm1_g"], p["w_all"])
    ya, k_buf, v_buf = _swa_sample(z_a, st["k"], st["v"], c["bias_s"], p["sink"], p["qg64"], p["kg"],
                                   c["bd"][0:A_KV_COLS, 0:A_KV_COLS])
    yb = _pool_sample(z_pool, st["pool"], p["w_pool"], p["pool_scale"])
    hc, c_new, n_new, m_new = _mlstm_sample(z_c[:, 0:3 * C_W], z_if, p["gate_bias"], st["c"], st["n"], st["m"])
    x2d = _merge(x2d, ya, yb, hc, z_c, z_g, p["hg"], p["wa"], p["wb"], p["wc"], p["wo"])
    x2d = _peer(x2d, p)
    pool_buf = jnp.concatenate([st["pool"][:, 1:, :], z_pool[:, None, :]], axis=1)
    return x2d.reshape(n, 1, D_MODEL), (k_buf, v_buf, pool_buf, c_new, n_new, m_new)


def kernel(x_prompt, x_sample, cache_k, cache_v, state_pool, state_mlstm_c, state_mlstm_n, state_mlstm_m, rel_bias, norm1_g, w_in, q_norm_g, k_norm_g, attn_sink, w_pool, pool_scale, b_igate, b_fgate, h_norm_g, w_br_a, w_br_b, w_br_c, w_out, norm2_g, peer_wq, peer_k1, peer_k2, peer_u, peer_v):
    a = dict(norm1_g=norm1_g, w_in=w_in, q_norm_g=q_norm_g, k_norm_g=k_norm_g, attn_sink=attn_sink, w_pool=w_pool,
             pool_scale=pool_scale, b_igate=b_igate, b_fgate=b_fgate, h_norm_g=h_norm_g, w_br_a=w_br_a,
             w_br_b=w_br_b, w_br_c=w_br_c, w_out=w_out, norm2_g=norm2_g, peer_wq=peer_wq, peer_k1=peer_k1,
             peer_k2=peer_k2, peer_u=peer_u, peer_v=peer_v)
    depth = w_in.shape[0]
    bd, tri, sel = _constants()
    bias_p, bias_s = _bias_tables(rel_bias)
    consts = dict(bd=bd, tri=tri, bias_p=bias_p, bias_s=bias_s)
    xp, xs = x_prompt, x_sample
    new_p = [[] for _ in range(6)]
    new_s = [[] for _ in range(6)]
    for l in range(depth):
        p = _layer_params(l, a)
        p["sel"] = sel
        p["selt"] = sel.T
        st_s = dict(k=cache_k[l], v=cache_v[l], pool=state_pool[l], c=state_mlstm_c[l], n=state_mlstm_n[l],
                    m=state_mlstm_m[l])
        xp, sp = _layer_prompt(xp, p, consts)
        xs, ss = _layer_sample(xs, st_s, p, consts)
        for i in range(6):
            new_p[i].append(sp[i])
            new_s[i].append(ss[i])
    outs_p = [jnp.stack(v) for v in new_p]
    outs_s = [jnp.stack(v) for v in new_s]
    return (xp, xs, *outs_p, *outs_s)
```
